```python
import jax, jax.numpy as jnp
from jax import lax
import numpy as np

D_MODEL = 1024
BATCH = 8
SEQ = 4096
DEPTH = 4

CTX_LEN = 256
GRID_W = 64
N_MOD = 9
D_FF = 2816
FFN_RES = 0.5

POOL_WIDTH = 256
POOL_WINDOWS = (2, 4, 8, 16)
POOL_GROUP = POOL_WIDTH // 4

DN_HEADS = 4
DN_DK = 64
DN_DV = 64
DN_CONV = 5
DN_CHUNK = 64
DN_QK = DN_HEADS * DN_DK
DN_VW = DN_HEADS * DN_DV
DN_CONV_CH = 2 * DN_QK + DN_VW

ATT_HEADS = 8
ATT_KV = 2
ATT_GROUP = ATT_HEADS // ATT_KV
ATT_HD = 64
ATT_BLOCK = 128
ROPE_BASE = 10000.0

N_BRANCH = 3
MIX_WIDTH = POOL_WIDTH + DN_VW + ATT_HEADS * ATT_HD
IN_SPLITS = (POOL_WIDTH, DN_CONV_CH, DN_VW, 4 * DN_HEADS,
             ATT_HEADS * ATT_HD, ATT_KV * ATT_HD, ATT_KV * ATT_HD, N_BRANCH * D_MODEL)
IN_COLS = POOL_WIDTH + DN_CONV_CH + DN_VW + 4 * DN_HEADS + ATT_HEADS * ATT_HD + 2 * ATT_KV * ATT_HD + N_BRANCH * D_MODEL

kernel_name = "hybrid_pool_deltanet_gqa_prefix_dit"

F32 = jnp.float32


def rmsnorm(x, g, eps=1e-6):
    xf = x.astype(F32)
    y = xf * lax.rsqrt(jnp.mean(xf * xf, axis=-1, keepdims=True) + eps)
    return (y * g.astype(F32)).astype(x.dtype)


def l2norm(x, eps=1e-6):
    xf = x.astype(F32)
    return xf * lax.rsqrt(jnp.sum(xf * xf, axis=-1, keepdims=True) + eps)


def modulate(h, shift, scale):
    return h * (1.0 + scale) + shift


def swiglu(h, w_in, w_out):
    gu = h @ w_in
    g, u = jnp.split(gu, 2, axis=-1)
    return (jax.nn.silu(g) * u) @ w_out


def ffn_sublayer(s, m, g_norm, w_in, w_out):
    h = modulate(rmsnorm(s, g_norm), m[0], m[1])
    return s + FFN_RES * m[2] * swiglu(h, w_in, w_out)


def split_cols(h):
    out, o = [], 0
    for w in IN_SPLITS:
        out.append(h[..., o:o + w])
        o += w
    return out


def multiscale_pool(p, w_grp, scale):
    B, L, _ = p.shape
    pf = p.astype(F32)
    cs = jnp.concatenate([jnp.zeros((B, 1, POOL_WIDTH), F32), jnp.cumsum(pf, axis=1)], axis=1)
    t = jnp.arange(L)
    groups = []
    for gi, w in enumerate(POOL_WINDOWS):
        lo = jnp.clip(t - w // 2, 0, L)
        hi = jnp.clip(t + w - w // 2, 0, L)
        sl = slice(gi * POOL_GROUP, (gi + 1) * POOL_GROUP)
        csg = cs[:, :, sl]
        win_sum = csg[:, hi] - csg[:, lo]
        cnt = (hi - lo).astype(F32)[None, :, None]
        groups.append(win_sum / cnt - pf[:, :, sl])
    y = jnp.stack(groups, axis=2).astype(p.dtype)
    y = jnp.einsum('blgi,gio->blgo', y, w_grp).reshape(B, L, POOL_WIDTH)
    return y * scale


def short_conv(x, w):
    C = x.shape[-1]
    y = lax.conv_general_dilated(x, w[:, None, :].astype(x.dtype), window_strides=(1,), padding='SAME',
                                 dimension_numbers=('NWC', 'WIO', 'NWC'), feature_group_count=C)
    return jax.nn.silu(y)


def dn_prepare(qkv, ba, conv_w, a_log, dt_bias):
    B, L, _ = qkv.shape
    qkv = short_conv(qkv, conv_w).astype(F32)
    q = qkv[..., :DN_QK].reshape(B, L, DN_HEADS, DN_DK)
    k = qkv[..., DN_QK:2 * DN_QK].reshape(B, L, DN_HEADS, DN_DK)
    v = qkv[..., 2 * DN_QK:].reshape(B, L, DN_HEADS, DN_DV)
    q = l2norm(q) * (DN_DK ** -0.5)
    k = l2norm(k)
    ba = ba.astype(F32).reshape(B, L, 2, 2, DN_HEADS)
    beta = jax.nn.sigmoid(ba[:, :, 0])
    g = -jnp.exp(a_log.astype(F32)) * jax.nn.softplus(ba[:, :, 1] + dt_bias.astype(F32))
    to_bhl = lambda t: jnp.transpose(t, (0, 2, 1, 3))
    beta = jnp.transpose(beta, (2, 0, 3, 1))
    g = jnp.transpose(g, (2, 0, 3, 1))
    return to_bhl(q), to_bhl(k), to_bhl(v), beta, g


def gated_delta_chunked(q, k, v, beta, g, S0):
    B, H, L, DK = q.shape
    DV = v.shape[-1]
    C = DN_CHUNK
    N = L // C
    q = q.reshape(B, H, N, C, DK)
    k = k.reshape(B, H, N, C, DK)
    v = v.reshape(B, H, N, C, DV)
    beta = beta.reshape(B, H, N, C)
    G = jnp.cumsum(g.reshape(B, H, N, C), axis=-1)
    tril_incl = jnp.tril(jnp.ones((C, C), dtype=bool))
    tril_strict = jnp.tril(jnp.ones((C, C), dtype=bool), -1)
    decay_mat = jnp.exp(jnp.where(tril_incl, G[..., :, None] - G[..., None, :], -jnp.inf))
    kk = jnp.einsum('bhncd,bhnsd->bhncs', k, k)
    A = jnp.where(tril_strict, beta[..., :, None] * kk * decay_mat, 0.0)
    lhs = A + jnp.eye(C, dtype=F32)
    rhs = jnp.concatenate([v * beta[..., None], k * (beta * jnp.exp(G))[..., None]], axis=-1)
    sol = lax.linalg.triangular_solve(lhs, rhs, left_side=True, lower=True, unit_diagonal=True)
    u, w = sol[..., :DV], sol[..., DV:]
    qk = jnp.einsum('bhncd,bhnsd->bhncs', q, k) * decay_mat
    q_dec = q * jnp.exp(G)[..., None]
    k_dec = k * jnp.exp(G[..., -1:] - G)[..., None]
    g_last = jnp.exp(G[..., -1])

    def step(S, xs):
        u_n, w_n, qk_n, qd_n, kd_n, gl_n = xs
        v_new = u_n - jnp.einsum('bhcd,bhde->bhce', w_n, S)
        o_n = jnp.einsum('bhcd,bhde->bhce', qd_n, S) + jnp.einsum('bhcs,bhse->bhce', qk_n, v_new)
        S = S * gl_n[..., None, None] + jnp.einsum('bhcd,bhce->bhde', kd_n, v_new)
        return S, o_n

    xs = tuple(jnp.moveaxis(t, 2, 0) for t in (u, w, qk, q_dec, k_dec, g_last))
    S_final, o = lax.scan(step, S0, xs)
    o = jnp.moveaxis(o, 0, 2).reshape(B, H, L, DV)
    return o, S_final


def bidir_delta(q, k, v, beta, g, S0_f, S0_b):
    flip = lambda t: jnp.flip(t, axis=2)
    o_f, S_f = gated_delta_chunked(q, k, v, beta[0], g[0], S0_f)
    o_b, S_b = gated_delta_chunked(flip(q), flip(k), flip(v), flip(beta[1]), flip(g[1]), S0_b)
    return o_f + flip(o_b), S_f, S_b


def dn_output(o, z, g_norm):
    B, H, L, DV = o.shape
    o = jnp.transpose(o, (0, 2, 1, 3))
    y = rmsnorm(o, g_norm) * jax.nn.silu(z.reshape(B, L, H, DV).astype(F32))
    return y.reshape(B, L, H * DV).astype(z.dtype)


def axial_rope_tables(n_tokens):
    rows = n_tokens // GRID_W
    row = jnp.broadcast_to(jnp.arange(rows)[:, None], (rows, GRID_W)).reshape(-1)
    col = jnp.broadcast_to(jnp.arange(GRID_W)[None, :], (rows, GRID_W)).reshape(-1)
    n_freq = ATT_HD // 4
    inv = ROPE_BASE ** (-jnp.arange(n_freq, dtype=F32) / n_freq)
    ang = jnp.concatenate([row[:, None].astype(F32) * inv, col[:, None].astype(F32) * inv], axis=-1)
    return jnp.cos(ang), jnp.sin(ang)


def apply_rope(x, cos, sin):
    xf = x.astype(F32)
    x1, x2 = xf[..., 0::2], xf[..., 1::2]
    c = cos[None, :, None, :]
    s = sin[None, :, None, :]
    y = jnp.stack([x1 * c - x2 * s, x1 * s + x2 * c], axis=-1).reshape(x.shape)
    return y.astype(x.dtype)


def gqa_softmax(q, k, v):
    s = jnp.einsum('bqkgd,bskd->bkgqs', q, k).astype(F32) * (ATT_HD ** -0.5)
    p = jax.nn.softmax(s, axis=-1).astype(v.dtype)
    return jnp.einsum('bkgqs,bskd->bqkgd', p, v)


def latent_attention(q, k_lat, v_lat, k_ctx, v_ctx):
    B, L = q.shape[:2]
    k_all = jnp.concatenate([k_ctx, k_lat], axis=1)
    v_all = jnp.concatenate([v_ctx, v_lat], axis=1)
    n_blk = L // ATT_BLOCK
    qb = q.reshape(B, n_blk, ATT_BLOCK, ATT_KV, ATT_GROUP, ATT_HD).transpose(1, 0, 2, 3, 4, 5)
    o = lax.map(lambda qblk: gqa_softmax(qblk, k_all, v_all), qb)
    return o.transpose(1, 0, 2, 3, 4, 5).reshape(B, L, ATT_HEADS * ATT_HD)


def context_attention(q, k, v):
    B, Lc = q.shape[:2]
    o = gqa_softmax(q.reshape(B, Lc, ATT_KV, ATT_GROUP, ATT_HD), k, v)
    return o.reshape(B, Lc, ATT_HEADS * ATT_HD)


def split_heads(t, n):
    return t.reshape(t.shape[0], t.shape[1], n, ATT_HD)


def merge_branches(y_pool, y_dn, y_att, gates, w_branch, w_out):
    ga, gb, gc = jnp.split(jax.nn.sigmoid(gates), N_BRANCH, axis=-1)
    wa = w_branch[:POOL_WIDTH]
    wb = w_branch[POOL_WIDTH:POOL_WIDTH + DN_VW]
    wc = w_branch[POOL_WIDTH + DN_VW:]
    m = ga * (y_pool @ wa) + gb * (y_dn @ wb) + gc * (y_att @ wc)
    return m @ w_out


def mixer_sublayer(x, xc, mx, mc, p, ctx_out):
    def project(s, m):
        h = modulate(rmsnorm(s, p["norm_mix"]), m[0], m[1])
        return split_cols(h @ p["w_in"])

    pool_x, qkv_x, z_x, ba_x, aq_x, ak_x, av_x, gate_x = project(x, mx)
    pool_c, qkv_c, z_c, ba_c, aq_c, ak_c, av_c, gate_c = project(xc, mc)

    B = x.shape[0]
    S0 = jnp.zeros((B, DN_HEADS, DN_DK, DN_DV), F32)
    qd_c, kd_c, vd_c, beta_c, g_c = dn_prepare(qkv_c, ba_c, p["dn_conv"], p["dn_a_log"], p["dn_dt_bias"])
    od_c, S_f, S_b = bidir_delta(qd_c, kd_c, vd_c, beta_c, g_c, S0, S0)
    qd_x, kd_x, vd_x, beta_x, g_x = dn_prepare(qkv_x, ba_x, p["dn_conv"], p["dn_a_log"], p["dn_dt_bias"])
    od_x, _, _ = bidir_delta(qd_x, kd_x, vd_x, beta_x, g_x, S_f, S_b)

    cos, sin = axial_rope_tables(x.shape[1])
    q_x = apply_rope(rmsnorm(split_heads(aq_x, ATT_HEADS), p["q_norm"]), cos, sin)
    k_x = apply_rope(rmsnorm(split_heads(ak_x, ATT_KV), p["k_norm"]), cos, sin)
    v_x = split_heads(av_x, ATT_KV)
    k_c = rmsnorm(split_heads(ak_c, ATT_KV), p["k_norm"])
    v_c = split_heads(av_c, ATT_KV)
    att_x = latent_attention(q_x, k_x, v_x, k_c, v_c)

    y_x = merge_branches(multiscale_pool(pool_x, p["pool_w"], p["pool_scale"]),
                         dn_output(od_x, z_x, p["dn_norm"]), att_x, gate_x, p["w_branch"], p["w_out"])
    x = x + mx[2] * y_x
    if ctx_out:
        q_c = rmsnorm(split_heads(aq_c, ATT_HEADS), p["q_norm"])
        att_c = context_attention(q_c, k_c, v_c)
        y_c = merge_branches(multiscale_pool(pool_c, p["pool_w"], p["pool_scale"]),
                             dn_output(od_c, z_c, p["dn_norm"]), att_c, gate_c, p["w_branch"], p["w_out"])
        xc = xc + mc[2] * y_c
    return x, xc


def trunk_layer(x, xc, mod, mod_c, p, ctx_out):
    mx = jnp.split(mod[:, None, :], N_MOD, axis=-1)
    mc = jnp.split(mod_c[:, None, :], N_MOD, axis=-1)
    x = ffn_sublayer(x, mx[0:3], p["norm_ffn1"], p["w_ffn1_in"], p["w_ffn1_out"])
    xc = ffn_sublayer(xc, mc[0:3], p["norm_ffn1"], p["w_ffn1_in"], p["w_ffn1_out"])
    x, xc = mixer_sublayer(x, xc, mx[3:6], mc[3:6], p, ctx_out)
    x = ffn_sublayer(x, mx[6:9], p["norm_ffn2"], p["w_ffn2_in"], p["w_ffn2_out"])
    if ctx_out:
        xc = ffn_sublayer(xc, mc[6:9], p["norm_ffn2"], p["w_ffn2_in"], p["w_ffn2_out"])
    return x, xc


def setup_inputs(seed: int = 0) -> dict:
    key = jax.random.key(seed)
    ks = iter(jax.random.split(key, 40))
    D = D_MODEL
    nrm = lambda shape, scale: jax.random.normal(next(ks), shape, F32) * scale
    gain = lambda shape: 1.0 + nrm(shape, 0.02)
    a_log = jnp.log(jax.random.uniform(next(ks), (DEPTH, 2, DN_HEADS), F32, 1.0, 16.0))
    dt = jnp.exp(jax.random.uniform(next(ks), (DEPTH, 2, DN_HEADS), F32, np.log(1e-3), np.log(1e-1)))
    dt_bias = dt + jnp.log(-jnp.expm1(-dt))
    w_branch = jnp.concatenate([nrm((DEPTH, POOL_WIDTH, D), POOL_WIDTH ** -0.5),
                                nrm((DEPTH, DN_VW, D), DN_VW ** -0.5),
                                nrm((DEPTH, ATT_HEADS * ATT_HD, D), (ATT_HEADS * ATT_HD) ** -0.5)], axis=1)
    return {
        "x": nrm((BATCH, SEQ, D), 1.0),
        "c": nrm((BATCH, D), 1.0),
        "ctx": nrm((BATCH, CTX_LEN, D), 1.0),
        "c_ctx": nrm((D,), 1.0),
        "w_ada": nrm((DEPTH, D, N_MOD * D), 0.5 * D ** -0.5),
        "b_ada": nrm((DEPTH, N_MOD * D), 0.02),
        "norm_ffn1": gain((DEPTH, D)),
        "w_ffn1_in": nrm((DEPTH, D, 2 * D_FF), D ** -0.5),
        "w_ffn1_out": nrm((DEPTH, D_FF, D), D_FF ** -0.5),
        "norm_mix": gain((DEPTH, D)),
        "w_in": nrm((DEPTH, D, IN_COLS), D ** -0.5),
        "pool_w": nrm((DEPTH, 4, POOL_GROUP, POOL_GROUP), POOL_GROUP ** -0.5),
        "pool_scale": gain((DEPTH, POOL_WIDTH)),
        "dn_conv": nrm((DEPTH, DN_CONV, DN_CONV_CH), DN_CONV ** -0.5),
        "dn_a_log": a_log,
        "dn_dt_bias": dt_bias,
        "dn_norm": gain((DEPTH, DN_DV)),
        "q_norm": gain((DEPTH, ATT_HD)),
        "k_norm": gain((DEPTH, ATT_HD)),
        "w_branch": w_branch,
        "w_out": nrm((DEPTH, D, D), D ** -0.5),
        "norm_ffn2": gain((DEPTH, D)),
        "w_ffn2_in": nrm((DEPTH, D, 2 * D_FF), D ** -0.5),
        "w_ffn2_out": nrm((DEPTH, D_FF, D), D_FF ** -0.5),
        "norm_final": gain((D,)),
    }


def reference(x, c, ctx, c_ctx, w_ada, b_ada, norm_ffn1, w_ffn1_in, w_ffn1_out, norm_mix, w_in,
              pool_w, pool_scale, dn_conv, dn_a_log, dn_dt_bias, dn_norm, q_norm, k_norm,
              w_branch, w_out, norm_ffn2, w_ffn2_in, w_ffn2_out, norm_final):
    s_lat = jax.nn.silu(c)
    s_ctx = jax.nn.silu(c_ctx)[None]
    xc = ctx
    for l in range(DEPTH):
        mod = s_lat @ w_ada[l] + b_ada[l]
        mod_c = s_ctx @ w_ada[l] + b_ada[l]
        p = {
            "norm_ffn1": norm_ffn1[l], "w_ffn1_in": w_ffn1_in[l], "w_ffn1_out": w_ffn1_out[l],
            "norm_mix": norm_mix[l], "w_in": w_in[l],
            "pool_w": pool_w[l], "pool_scale": pool_scale[l],
            "dn_conv": dn_conv[l], "dn_a_log": dn_a_log[l], "dn_dt_bias": dn_dt_bias[l], "dn_norm": dn_norm[l],
            "q_norm": q_norm[l], "k_norm": k_norm[l],
            "w_branch": w_branch[l], "w_out": w_out[l],
            "norm_ffn2": norm_ffn2[l], "w_ffn2_in": w_ffn2_in[l], "w_ffn2_out": w_ffn2_out[l],
        }
        x, xc = trunk_layer(x, xc, mod, mod_c, p, ctx_out=(l < DEPTH - 1))
    return rmsnorm(x, norm_final)
```

```python
import functools
import math

import jax
import jax.numpy as jnp
import numpy as np
from jax import lax
from jax.experimental import pallas as pl
from jax.experimental.pallas import tpu as pltpu

F32 = jnp.float32
BF16 = jnp.bfloat16

N_MOD = 9
FFN_RES = 0.5
POOL_WIDTH = 256
POOL_WINDOWS = (2, 4, 8, 16)
POOL_GROUP = 64
DN_HEADS = 4
DN_DK = 64
DN_DV = 64
DN_CONV = 5
DN_CHUNK = 64
DN_QK = DN_HEADS * DN_DK
DN_VW = DN_HEADS * DN_DV
DN_CONV_CH = 2 * DN_QK + DN_VW
ATT_HEADS = 8
ATT_KV = 2
ATT_GROUP = ATT_HEADS // ATT_KV
ATT_HD = 64
ROPE_BASE = 10000.0
GRID_W = 64
N_BRANCH = 3
EPS = 1e-6
LOG2E = 1.4426950408889634

LANE = 128
SUBLANE = 8
VMEM_LIMIT = 56 * 1024 * 1024

BA_PAD = LANE
Q_W = ATT_HEADS * ATT_HD
KV_W = ATT_KV * ATT_HD
C_POOL = 0
C_QKV = C_POOL + POOL_WIDTH
C_Z = C_QKV + DN_CONV_CH
C_AQ = C_Z + DN_VW
C_AK = C_AQ + Q_W
C_AV = C_AK + KV_W
C_GATE = C_AV + KV_W


def _params(*sem):
    return pltpu.CompilerParams(dimension_semantics=sem, vmem_limit_bytes=VMEM_LIMIT)


def _resident(shape):
    nd = len(shape)
    return pl.BlockSpec(shape, lambda *_: (0,) * nd, pipeline_mode=pl.Buffered(1))


def _silu(x):
    return x * jax.nn.sigmoid(x)


def _dot(a, b):
    return jnp.dot(a, b, preferred_element_type=F32)


def _dot_nt(a, b):
    return lax.dot_general(a, b, (((1,), (1,)), ((), ())), preferred_element_type=F32)


def _dot_tn(a, b):
    return lax.dot_general(a, b, (((0,), (0,)), ((), ())), preferred_element_type=F32)


def _split3(x):
    hi = x.astype(BF16)
    r1 = x - hi.astype(F32)
    mid = r1.astype(BF16)
    lo = (r1 - mid.astype(F32)).astype(BF16)
    return hi, mid, lo


def _dot_exact_lhs(a_bf16, x_f32):
    hi, mid, lo = _split3(x_f32)
    return _dot(a_bf16, hi) + _dot(a_bf16, mid) + _dot(a_bf16, lo)


def _ada_kernel(s_ref, w_ref, b_ref, o_ref):
    s = _silu(s_ref[...])
    w = w_ref[0]
    s_hi, s_mid, s_lo = _split3(s)
    w_hi, w_mid, w_lo = _split3(w)
    acc = _dot(s_hi, w_hi)
    acc = acc + _dot(s_hi, w_mid) + _dot(s_mid, w_hi)
    acc = acc + _dot(s_hi, w_lo) + _dot(s_mid, w_mid) + _dot(s_lo, w_hi)
    o_ref[0] = acc + b_ref[0]


def _ada_call(cond, w_ada, b_ada):
    depth, d, n = w_ada.shape
    rows = cond.shape[0]
    tn = 1024
    return pl.pallas_call(
        _ada_kernel,
        out_shape=jax.ShapeDtypeStruct((depth, rows, n), F32),
        grid=(depth, n // tn),
        in_specs=[
            pl.BlockSpec((rows, d), lambda l, j: (0, 0)),
            pl.BlockSpec((1, d, tn), lambda l, j: (l, 0, j)),
            pl.BlockSpec((1, 1, tn), lambda l, j: (l, 0, j)),
        ],
        out_specs=pl.BlockSpec((1, rows, tn), lambda l, j: (l, 0, j)),
        compiler_params=_params("parallel", "parallel"),
        name="ada_mod",
    )(cond, w_ada, b_ada.reshape(depth, 1, n))


def _norm_mod(x, gain, shift, scale):
    ms = jnp.mean(x * x, axis=-1, keepdims=True)
    h = x * lax.rsqrt(ms + EPS) * gain
    return h * (1.0 + scale) + shift


def _ffn_kernel(x_ref, mod_ref, g_ref, wi_ref, wo_ref, o_ref, acc_ref, *, d_ff, tf):
    x = x_ref[0]
    h = _norm_mod(x, g_ref[...], mod_ref[0, 0], mod_ref[0, 1]).astype(BF16)
    for j in range(d_ff // tf):
        g = _dot(h, wi_ref[:, j * tf:(j + 1) * tf])
        u = _dot(h, wi_ref[:, d_ff + j * tf:d_ff + (j + 1) * tf])
        a = (_silu(g) * u).astype(BF16)
        part = _dot(a, wo_ref[j * tf:(j + 1) * tf, :])
        if j == 0:
            acc_ref[...] = part
        else:
            acc_ref[...] += part
    o_ref[0] = x + (FFN_RES * mod_ref[0, 2]) * acc_ref[...]


def _ffn_call(x, mod, sub, gain, w_in, w_out):
    nb, t, d = x.shape
    d_ff = w_out.shape[0]
    tm = min(512, t)
    tf = 256
    kern = functools.partial(_ffn_kernel, d_ff=d_ff, tf=tf)
    return pl.pallas_call(
        kern,
        out_shape=jax.ShapeDtypeStruct(x.shape, F32),
        grid=(nb, t // tm),
        in_specs=[
            pl.BlockSpec((1, tm, d), lambda b, i: (b, i, 0)),
            pl.BlockSpec((1, 3, 1, d), lambda b, i: (b, sub, 0, 0)),
            _resident((1, d)),
            _resident((d, 2 * d_ff)),
            _resident((d_ff, d)),
        ],
        out_specs=pl.BlockSpec((1, tm, d), lambda b, i: (b, i, 0)),
        scratch_shapes=[pltpu.VMEM((tm, d), F32)],
        compiler_params=_params("parallel", "parallel"),
        name="ffn",
    )(x, mod, gain.reshape(1, d), w_in, w_out)


def _head_norm_rope(t_raw, gain, e_mat, p_mat, cos, sin, out_scale):
    ms = _dot((t_raw * t_raw).astype(BF16), e_mat)
    r = lax.rsqrt(ms + EPS) * out_scale
    t = t_raw * gain
    t_sw = _dot(t.astype(BF16), p_mat)
    w = t_raw.shape[1]
    outs = []
    for g in range(w // LANE):
        sl = slice(g * LANE, (g + 1) * LANE)
        outs.append(r[:, sl] * (t[:, sl] * cos + t_sw[:, sl] * sin))
    return jnp.concatenate(outs, axis=1) if len(outs) > 1 else outs[0]


def _proj_kernel(x_ref, mod_ref, g_ref, w_ref, cos_ref, sin_ref, qg_ref, kg_ref,
                 eq_ref, pq_ref, ek_ref, pk_ref,
                 pool_ref, qkv_ref, z_ref, ba_ref, q_ref, k_ref, v_ref, gate_ref, *, n_gate):
    x = x_ref[0]
    h = _norm_mod(x, g_ref[...], mod_ref[0, 0], mod_ref[0, 1]).astype(BF16)

    def cols(lo, width):
        return _dot(h, w_ref[:, lo:lo + width])

    pool_ref[0] = cols(C_POOL, POOL_WIDTH)
    for j in range(DN_CONV_CH // 256):
        qkv_ref[0, :, j * 256:(j + 1) * 256] = cols(C_QKV + j * 256, 256)
    z_ref[0] = cols(C_Z, DN_VW).astype(BF16)
    cos = cos_ref[...]
    sin = sin_ref[...]
    q = _head_norm_rope(cols(C_AQ, Q_W), qg_ref[...], eq_ref[...], pq_ref[...], cos, sin,
                        (ATT_HD ** -0.5) * LOG2E)
    q_ref[0] = q.astype(BF16)
    k = _head_norm_rope(cols(C_AK, KV_W), kg_ref[...], ek_ref[...], pk_ref[...], cos, sin, 1.0)
    k_ref[0] = k.astype(BF16)
    v_ref[0] = cols(C_AV, KV_W).astype(BF16)
    tg = 512
    for j in range(n_gate // tg):
        gate_ref[0, :, j * tg:(j + 1) * tg] = jax.nn.sigmoid(cols(C_GATE + j * tg, tg)).astype(BF16)
    ba_ref[0] = cols(C_GATE + n_gate, BA_PAD)


def _proj_call(x, mod, sub, gain, w_perm, cos, sin, qg, kg, eq, pq, ek, pk):
    nb, t, d = x.shape
    n_gate = N_BRANCH * d
    ncol = w_perm.shape[1]
    tm = min(512, t)
    nt = t // tm
    row = lambda w: pl.BlockSpec((1, tm, w), lambda b, i: (b, i, 0))
    out_shapes = [
        jax.ShapeDtypeStruct((nb, t, POOL_WIDTH), F32),
        jax.ShapeDtypeStruct((nb, t, DN_CONV_CH), F32),
        jax.ShapeDtypeStruct((nb, t, DN_VW), BF16),
        jax.ShapeDtypeStruct((nb, t, BA_PAD), F32),
        jax.ShapeDtypeStruct((nb, t, Q_W), BF16),
        jax.ShapeDtypeStruct((nb, t, KV_W), BF16),
        jax.ShapeDtypeStruct((nb, t, KV_W), BF16),
        jax.ShapeDtypeStruct((nb, t, n_gate), BF16),
    ]
    out_specs = [row(POOL_WIDTH), row(DN_CONV_CH), row(DN_VW), row(BA_PAD), row(Q_W), row(KV_W),
                 row(KV_W), row(n_gate)]
    kern = functools.partial(_proj_kernel, n_gate=n_gate)
    return pl.pallas_call(
        kern,
        out_shape=out_shapes,
        grid=(nb, nt),
        in_specs=[
            pl.BlockSpec((1, tm, d), lambda b, i: (b, i, 0)),
            pl.BlockSpec((1, 3, 1, d), lambda b, i: (b, sub, 0, 0)),
            _resident((1, d)),
            _resident((d, ncol)),
            pl.BlockSpec((tm, LANE), lambda b, i: (i, 0)),
            pl.BlockSpec((tm, LANE), lambda b, i: (i, 0)),
            _resident((1, Q_W)),
            _resident((1, KV_W)),
            _resident((Q_W, Q_W)),
            _resident((Q_W, Q_W)),
            _resident((KV_W, KV_W)),
            _resident((KV_W, KV_W)),
        ],
        out_specs=out_specs,
        compiler_params=_params("parallel", "parallel"),
        name="mix_proj",
    )(x, mod, gain.reshape(1, d), w_perm, cos, sin, qg, kg, eq, pq, ek, pk)


def _fill_halo(pad_ref, prev_ref, cur_ref, next_ref, tt):
    i = pl.program_id(1)
    n = pl.num_programs(1)
    prev = prev_ref[0]
    nxt = next_ref[0]
    pad_ref[0:SUBLANE, :] = jnp.where(i > 0, prev, jnp.zeros_like(prev))
    pad_ref[SUBLANE:SUBLANE + tt, :] = cur_ref[0]
    pad_ref[SUBLANE + tt:2 * SUBLANE + tt, :] = jnp.where(i < n - 1, nxt, jnp.zeros_like(nxt))


def _halo_specs(tt, width, n_rows):
    per = tt // SUBLANE
    last = n_rows // SUBLANE - 1
    return [
        pl.BlockSpec((1, SUBLANE, width), lambda b, i: (b, jnp.maximum(i * per - 1, 0), 0)),
        pl.BlockSpec((1, tt, width), lambda b, i: (b, i, 0)),
        pl.BlockSpec((1, SUBLANE, width), lambda b, i: (b, jnp.minimum((i + 1) * per, last), 0)),
    ]


def _pool_kernel(prev_ref, cur_ref, next_ref, w_ref, s_ref, o_ref, pad_ref, *, tt, seq):
    _fill_halo(pad_ref, prev_ref, cur_ref, next_ref, tt)
    i = pl.program_id(1)
    t = i * tt + lax.broadcasted_iota(jnp.int32, (tt, 1), 0)
    lane = lax.broadcasted_iota(jnp.int32, (1, POOL_WIDTH), 1)
    cur = pad_ref[SUBLANE:SUBLANE + tt, :]
    y = jnp.zeros((tt, POOL_WIDTH), F32)
    for gi, w in enumerate(POOL_WINDOWS):
        acc = None
        for off in range(-(w // 2), w - w // 2):
            term = pad_ref[SUBLANE + off:SUBLANE + off + tt, :]
            acc = term if acc is None else acc + term
        lo = jnp.maximum(t - w // 2, 0)
        hi = jnp.minimum(t + w - w // 2, seq)
        cnt = (hi - lo).astype(F32)
        in_group = (lane >= gi * POOL_GROUP) & (lane < (gi + 1) * POOL_GROUP)
        y = jnp.where(in_group, acc / cnt - cur, y)
    o_ref[0] = (_dot(y.astype(BF16), w_ref[...]) * s_ref[...]).astype(BF16)


def _pool_call(p, w_bd, scale):
    nb, t, c = p.shape
    tt = min(512, t)
    kern = functools.partial(_pool_kernel, tt=tt, seq=t)
    return pl.pallas_call(
        kern,
        out_shape=jax.ShapeDtypeStruct((nb, t, c), BF16),
        grid=(nb, t // tt),
        in_specs=_halo_specs(tt, c, t) + [_resident((c, c)), _resident((1, c))],
        out_specs=pl.BlockSpec((1, tt, c), lambda b, i: (b, i, 0)),
        scratch_shapes=[pltpu.VMEM((tt + 2 * SUBLANE, c), F32)],
        compiler_params=_params("parallel", "arbitrary"),
        name="pool_mix",
    )(p, p, p, w_bd, scale)


def _conv_kernel(prev_ref, cur_ref, next_ref, w_ref, e_ref, q_ref, k_ref, v_ref, pad_ref, *, tt):
    _fill_halo(pad_ref, prev_ref, cur_ref, next_ref, tt)
    half = DN_CONV // 2
    outs = (q_ref, k_ref, v_ref)
    for part in range(3):
        cs = slice(part * DN_QK, (part + 1) * DN_QK)
        acc = None
        for j in range(DN_CONV):
            lo = SUBLANE + j - half
            term = pad_ref[lo:lo + tt, cs] * w_ref[j:j + 1, cs]
            acc = term if acc is None else acc + term
        y = _silu(acc)
        if part < 2:
            ss = _dot((y * y).astype(BF16), e_ref[...])
            y = y * lax.rsqrt(ss + EPS)
            if part == 0:
                y = y * (DN_DK ** -0.5)
        outs[part][0] = y.astype(BF16)


def _conv_call(qkv, conv_w, e_sum):
    nb, t, c = qkv.shape
    tt = min(512, t)
    kern = functools.partial(_conv_kernel, tt=tt)
    o = jax.ShapeDtypeStruct((nb, t, DN_QK), BF16)
    ospec = pl.BlockSpec((1, tt, DN_QK), lambda b, i: (b, i, 0))
    return pl.pallas_call(
        kern,
        out_shape=[o, o, o],
        grid=(nb, t // tt),
        in_specs=_halo_specs(tt, c, t) + [_resident((DN_CONV, c)), _resident((DN_QK, DN_QK))],
        out_specs=[ospec, ospec, ospec],
        scratch_shapes=[pltpu.VMEM((tt + 2 * SUBLANE, c), F32)],
        compiler_params=_params("parallel", "arbitrary"),
        name="dn_conv",
    )(qkv, qkv, qkv, conv_w, e_sum)


def _delta_kernel(qc_ref, kc_ref, vc_ref, bac_ref, zc_ref, qx_ref, kx_ref, vx_ref, bax_ref, zx_ref,
                  alog_ref, dtb_ref, gn_ref, e_ref, yc_ref, yx_ref,
                  sf_ref, sb_ref, oc_ref, ox_ref, *, n_ctx, n_lat):
    c = DN_CHUNK
    w = DN_HEADS * c
    row = lax.broadcasted_iota(jnp.int32, (w, w), 0)
    col = lax.broadcasted_iota(jnp.int32, (w, w), 1)
    same = (row // c) == (col // c)
    rp = row % c
    cp = col % c
    eye = jnp.where(row == col, 1.0, 0.0).astype(F32)
    m_incl = (same & (rp >= cp), same & (rp <= cp))
    m_strict = (same & (rp > cp), same & (rp < cp))
    ones_bd = jnp.where(same, 1.0, 0.0).astype(BF16)
    m_incl_b = tuple(jnp.where(m, 1.0, 0.0).astype(BF16) for m in m_incl)
    m_strict_f = tuple(jnp.where(m, 1.0, 0.0).astype(F32) for m in m_strict)
    lane = lax.broadcasted_iota(jnp.int32, (w, LANE), 1)
    rhead = lax.broadcasted_iota(jnp.int32, (w, LANE), 0) // c
    neg_a = -jnp.exp(alog_ref[...])
    dtb = dtb_ref[...]

    sf_ref[...] = jnp.zeros_like(sf_ref)
    sb_ref[...] = jnp.zeros_like(sb_ref)
    oc_ref[...] = jnp.zeros_like(oc_ref)
    ox_ref[...] = jnp.zeros_like(ox_ref)

    def tile4(a):
        return jnp.concatenate([a, a, a, a], axis=0)

    def one_direction(d, q_bd, k_bd, v_bd, kk, qk, ba4, s_ref):
        beta_all = jax.nn.sigmoid(ba4)
        g_all = neg_a * jax.nn.softplus(ba4 + dtb)
        beta = jnp.sum(jnp.where(lane == d * DN_HEADS + rhead, beta_all, 0.0), axis=1, keepdims=True)
        g = jnp.sum(jnp.where(lane == 2 * DN_HEADS + d * DN_HEADS + rhead, g_all, 0.0), axis=1,
                    keepdims=True)
        g_wide = jnp.broadcast_to(g, (w, LANE))
        g_incl = _dot_exact_lhs(m_incl_b[d], g_wide)[:, :1]
        g_tot = _dot_exact_lhs(ones_bd, g_wide)[:, :1]
        diff = _dot_exact_lhs(m_incl_b[d], g * m_strict_f[d])
        decay = jnp.where(m_incl[d], jnp.exp(diff), 0.0)
        e_g = jnp.exp(g_incl)
        e_rest = jnp.exp(g_tot - g_incl)
        e_tot = jnp.exp(g_tot)
        a = jnp.where(m_strict[d], beta * kk * decay, 0.0)
        x = -a
        t = eye + x
        p = x.astype(BF16)
        for _ in range(5):
            p32 = _dot(p, p)
            p = p32.astype(BF16)
            t = t + _dot(t.astype(BF16), p)
        t_b = t.astype(BF16)
        u = _dot(t_b, (v_bd * beta).astype(BF16))
        wm = _dot(t_b, (k_bd * (beta * e_g)).astype(BF16))
        s_old = s_ref[...]
        s_b = s_old.astype(BF16)
        v_new = u - _dot(wm.astype(BF16), s_b)
        v_new_b = v_new.astype(BF16)
        o = _dot((q_bd * e_g).astype(BF16), s_b) + _dot((qk * decay).astype(BF16), v_new_b)
        s_ref[...] = s_old * e_tot + _dot_tn((k_bd * e_rest).astype(BF16), v_new_b)
        return o[0:c] + o[c:2 * c] + o[2 * c:3 * c] + o[3 * c:4 * c]

    def chunk_step(refs, o_ref, rows_f, rows_b):
        q_ref, k_ref, v_ref, ba_ref = refs
        for d, rows in ((0, rows_f), (1, rows_b)):
            q_bd = jnp.where(same, tile4(q_ref[0, rows, :].astype(F32)), 0.0)
            k_bd = jnp.where(same, tile4(k_ref[0, rows, :].astype(F32)), 0.0)
            v_bd = jnp.where(same, tile4(v_ref[0, rows, :].astype(F32)), 0.0)
            k_b = k_bd.astype(BF16)
            kk = _dot_nt(k_b, k_b)
            qk = _dot_nt(q_bd.astype(BF16), k_b)
            ba4 = tile4(ba_ref[0, rows, :])
            o_tok = one_direction(d, q_bd, k_bd, v_bd, kk, qk, ba4, sf_ref if d == 0 else sb_ref)
            o_ref[rows, :] += o_tok

    ctx_refs = (qc_ref, kc_ref, vc_ref, bac_ref)
    lat_refs = (qx_ref, kx_ref, vx_ref, bax_ref)
    for i in range(n_ctx):
        chunk_step(ctx_refs, oc_ref, pl.ds(i * c, c), pl.ds((n_ctx - 1 - i) * c, c))

    def lat_body(i, carry):
        f0 = pl.multiple_of(i * c, c)
        b0 = pl.multiple_of((n_lat - 1 - i) * c, c)
        chunk_step(lat_refs, ox_ref, pl.ds(f0, c), pl.ds(b0, c))
        return carry

    lax.fori_loop(0, n_lat, lat_body, 0)

    def finish(o_ref, z_ref, y_ref, n_rows):
        tt = min(512, n_rows)
        for j in range(n_rows // tt):
            rs = slice(j * tt, (j + 1) * tt)
            o = o_ref[rs, :]
            ms = _dot((o * o).astype(BF16), e_ref[...])
            y = o * lax.rsqrt(ms + EPS) * gn_ref[...]
            y_ref[0, rs, :] = (y * _silu(z_ref[0, rs, :].astype(F32))).astype(BF16)

    finish(oc_ref, zc_ref, yc_ref, n_ctx * c)
    finish(ox_ref, zx_ref, yx_ref, n_lat * c)


def _delta_call(ctx_parts, lat_parts, alog, dtb, gn, e_mean):
    qc, kc, vc, bac, zc = ctx_parts
    qx, kx, vx, bax, zx = lat_parts
    nb, lc, wd = qc.shape
    lx = qx.shape[1]
    kern = functools.partial(_delta_kernel, n_ctx=lc // DN_CHUNK, n_lat=lx // DN_CHUNK)
    seq = lambda n, width: pl.BlockSpec((1, n, width), lambda b: (b, 0, 0))
    in_specs = [seq(lc, wd), seq(lc, wd), seq(lc, wd), seq(lc, BA_PAD), seq(lc, wd),
                seq(lx, wd), seq(lx, wd), seq(lx, wd), seq(lx, BA_PAD), seq(lx, wd),
                _resident((1, LANE)), _resident((1, LANE)), _resident((1, wd)), _resident((wd, wd))]
    return pl.pallas_call(
        kern,
        out_shape=[jax.ShapeDtypeStruct((nb, lc, wd), BF16), jax.ShapeDtypeStruct((nb, lx, wd), BF16)],
        grid=(nb,),
        in_specs=in_specs,
        out_specs=[seq(lc, wd), seq(lx, wd)],
        scratch_shapes=[pltpu.VMEM((wd, wd), F32), pltpu.VMEM((wd, wd), F32),
                        pltpu.VMEM((lc, wd), F32), pltpu.VMEM((lx, wd), F32)],
        compiler_params=_params("parallel"),
        name="delta_rule",
    )(qc, kc, vc, bac, zc, qx, kx, vx, bax, zx, alog, dtb, gn, e_mean)


def _attn_kernel(q_ref, kt_ref, v_ref, o_ref, qs_ref, *, tq, tk, n_k):
    for h in range(ATT_GROUP):
        qs_ref[h * tq:(h + 1) * tq, :] = q_ref[0, :, h * ATT_HD:(h + 1) * ATT_HD]
    qs = qs_ref[...]
    rows = ATT_GROUP * tq

    def body(j, carry):
        m, acc = carry
        k0 = pl.multiple_of(j * tk, tk)
        s = _dot(qs, kt_ref[0, 0, :, pl.ds(k0, tk)])
        m_new = jnp.maximum(m, jnp.max(s, axis=1, keepdims=True))
        p = jnp.exp2(s - m_new)
        acc = acc * jnp.exp2(m - m_new) + _dot(p.astype(BF16), v_ref[0, 0, pl.ds(k0, tk), :])
        return m_new, acc

    m0 = jnp.full((rows, 1), -1e30, F32)
    acc0 = jnp.zeros((rows, LANE), F32)
    _, acc = lax.fori_loop(0, n_k, body, (m0, acc0))
    out = acc[:, :ATT_HD] / acc[:, ATT_HD:ATT_HD + 1]
    o_ref[0] = jnp.concatenate([out[h * tq:(h + 1) * tq] for h in range(ATT_GROUP)], axis=1).astype(BF16)


def _attn_call(q, kt, v_aug):
    nb, lq, _ = q.shape
    s = kt.shape[3]
    tq = min(128, lq)
    tk = 256
    gw = ATT_GROUP * ATT_HD
    kern = functools.partial(_attn_kernel, tq=tq, tk=tk, n_k=s // tk)
    return pl.pallas_call(
        kern,
        out_shape=jax.ShapeDtypeStruct((nb, lq, Q_W), BF16),
        grid=(nb, ATT_KV, lq // tq),
        in_specs=[
            pl.BlockSpec((1, tq, gw), lambda b, g, i: (b, i, g)),
            pl.BlockSpec((1, 1, ATT_HD, s), lambda b, g, i: (b, g, 0, 0)),
            pl.BlockSpec((1, 1, s, LANE), lambda b, g, i: (b, g, 0, 0)),
        ],
        out_specs=pl.BlockSpec((1, tq, gw), lambda b, g, i: (b, i, g)),
        scratch_shapes=[pltpu.VMEM((ATT_GROUP * tq, ATT_HD), BF16)],
        compiler_params=_params("parallel", "parallel", "arbitrary"),
        name="gqa_attn",
    )(q, kt, v_aug)


def _merge_kernel(x_ref, mod_ref, yp_ref, yd_ref, ya_ref, gate_ref, wb_ref, wo_ref, o_ref):
    d = x_ref.shape[2]
    a0 = POOL_WIDTH
    a1 = POOL_WIDTH + DN_VW
    m = gate_ref[0, :, 0:d].astype(F32) * _dot(yp_ref[0], wb_ref[0:a0, :])
    m = m + gate_ref[0, :, d:2 * d].astype(F32) * _dot(yd_ref[0], wb_ref[a0:a1, :])
    m = m + gate_ref[0, :, 2 * d:3 * d].astype(F32) * _dot(ya_ref[0], wb_ref[a1:, :])
    o_ref[0] = x_ref[0] + mod_ref[0, 2] * _dot(m.astype(BF16), wo_ref[...])


def _merge_call(x, mod, sub, y_pool, y_dn, y_att, gate, w_branch, w_out):
    nb, t, d = x.shape
    tm = min(512, t)
    row = lambda w: pl.BlockSpec((1, tm, w), lambda b, i: (b, i, 0))
    return pl.pallas_call(
        _merge_kernel,
        out_shape=jax.ShapeDtypeStruct(x.shape, F32),
        grid=(nb, t // tm),
        in_specs=[row(d), pl.BlockSpec((1, 3, 1, d), lambda b, i: (b, sub, 0, 0)),
                  row(POOL_WIDTH), row(DN_VW), row(Q_W), row(N_BRANCH * d),
                  _resident(w_branch.shape), _resident(w_out.shape)],
        out_specs=row(d),
        compiler_params=_params("parallel", "parallel"),
        name="merge",
    )(x, mod, y_pool, y_dn, y_att, gate, w_branch, w_out)


def _final_norm_kernel(x_ref, g_ref, o_ref):
    x = x_ref[0]
    ms = jnp.mean(x * x, axis=-1, keepdims=True)
    o_ref[0] = x * lax.rsqrt(ms + EPS) * g_ref[...]


def _final_norm_call(x, gain):
    nb, t, d = x.shape
    tm = min(1024, t)
    return pl.pallas_call(
        _final_norm_kernel,
        out_shape=jax.ShapeDtypeStruct(x.shape, F32),
        grid=(nb, t // tm),
        in_specs=[pl.BlockSpec((1, tm, d), lambda b, i: (b, i, 0)), _resident((1, d))],
        out_specs=pl.BlockSpec((1, tm, d), lambda b, i: (b, i, 0)),
        compiler_params=_params("parallel", "parallel"),
        name="final_norm",
    )(x, gain.reshape(1, d))


def _block_diag_const(width, block, value):
    idx = np.arange(width)
    return jnp.asarray(np.where((idx[:, None] // block) == (idx[None, :] // block), value, 0.0), BF16)


def _pair_swap_const(width):
    idx = np.arange(width)
    return jnp.asarray((idx[:, None] == (idx[None, :] ^ 1)).astype(np.float32), BF16)


def _rope_tables(n_tokens):
    rows = n_tokens // GRID_W
    row = jnp.broadcast_to(jnp.arange(rows)[:, None], (rows, GRID_W)).reshape(-1)
    col = jnp.broadcast_to(jnp.arange(GRID_W)[None, :], (rows, GRID_W)).reshape(-1)
    n_freq = ATT_HD // 4
    inv = ROPE_BASE ** (-jnp.arange(n_freq, dtype=F32) / n_freq)
    ang = jnp.concatenate([row[:, None].astype(F32) * inv, col[:, None].astype(F32) * inv], axis=-1)
    cos = jnp.repeat(jnp.cos(ang), 2, axis=1)
    sin = jnp.repeat(jnp.sin(ang), 2, axis=1)
    sign = jnp.where(jnp.arange(ATT_HD) % 2 == 0, -1.0, 1.0).astype(F32)
    sin = sin * sign
    reps = LANE // ATT_HD
    return jnp.tile(cos, (1, reps)), jnp.tile(sin, (1, reps))


def _permute_w_in(w_in):
    d = w_in.shape[0]
    o_pool = 0
    o_qkv = o_pool + POOL_WIDTH
    o_z = o_qkv + DN_CONV_CH
    o_ba = o_z + DN_VW
    o_aq = o_ba + 4 * DN_HEADS
    o_gate = o_aq + Q_W + 2 * KV_W
    pad = jnp.zeros((d, BA_PAD - 4 * DN_HEADS), w_in.dtype)
    return jnp.concatenate([w_in[:, o_pool:o_ba], w_in[:, o_aq:], w_in[:, o_ba:o_aq], pad], axis=1).astype(BF16)


def _lane_vec(vals, offset):
    out = jnp.zeros((1, LANE), F32)
    return out.at[0, offset:offset + vals.size].set(vals.reshape(-1).astype(F32))


def _kv_layout(k_c, v_c, k_x=None, v_x=None):
    k = k_c if k_x is None else jnp.concatenate([k_c, k_x], axis=1)
    v = v_c if v_x is None else jnp.concatenate([v_c, v_x], axis=1)
    nb, s, _ = k.shape
    kt = k.reshape(nb, s, ATT_KV, ATT_HD).transpose(0, 2, 3, 1)
    v4 = v.reshape(nb, s, ATT_KV, ATT_HD).transpose(0, 2, 1, 3)
    ones = jnp.ones((nb, ATT_KV, s, 1), v4.dtype)
    zeros = jnp.zeros((nb, ATT_KV, s, LANE - ATT_HD - 1), v4.dtype)
    return kt, jnp.concatenate([v4, ones, zeros], axis=3)


def kernel(x, c, ctx, c_ctx, w_ada, b_ada, norm_ffn1, w_ffn1_in, w_ffn1_out, norm_mix, w_in, pool_w, pool_scale, dn_conv, dn_a_log, dn_dt_bias, dn_norm, q_norm, k_norm, w_branch, w_out, norm_ffn2, w_ffn2_in, w_ffn2_out, norm_final):
    nb, seq, d = x.shape
    lc = ctx.shape[1]
    depth = w_ada.shape[0]

    n_rows = -(-(nb + 1) // SUBLANE) * SUBLANE
    cond = jnp.concatenate([c, c_ctx[None], jnp.zeros((n_rows - nb - 1, d), F32)], axis=0)
    mod_all = _ada_call(cond, w_ada, b_ada).reshape(depth, n_rows, N_MOD, 1, d)

    cos_x, sin_x = _rope_tables(seq)
    cos_c = jnp.ones((nb * lc, LANE), F32)
    sin_c = jnp.zeros((nb * lc, LANE), F32)
    e_q = _block_diag_const(Q_W, ATT_HD, 1.0 / ATT_HD)
    e_k = _block_diag_const(KV_W, ATT_HD, 1.0 / ATT_HD)
    p_q = _pair_swap_const(Q_W)
    p_k = _pair_swap_const(KV_W)
    e_sum = _block_diag_const(DN_QK, DN_DK, 1.0)
    e_mean = _block_diag_const(DN_VW, DN_DV, 1.0 / DN_DV)

    xc = ctx.reshape(1, nb * lc, d)
    for l in range(depth):
        ctx_out = l < depth - 1
        mod_x = mod_all[l, :nb]
        mod_c = mod_all[l, nb:nb + 1]
        wf1_in = w_ffn1_in[l].astype(BF16)
        wf1_out = w_ffn1_out[l].astype(BF16)
        wf2_in = w_ffn2_in[l].astype(BF16)
        wf2_out = w_ffn2_out[l].astype(BF16)
        w_perm = _permute_w_in(w_in[l])
        w_br = w_branch[l].astype(BF16)
        w_o = w_out[l].astype(BF16)
        qg = jnp.tile(q_norm[l], ATT_HEADS).reshape(1, Q_W)
        kg = jnp.tile(k_norm[l], ATT_KV).reshape(1, KV_W)
        pool_bd = jax.scipy.linalg.block_diag(*[pool_w[l, g] for g in range(len(POOL_WINDOWS))]).astype(BF16)
        pool_s = pool_scale[l].reshape(1, POOL_WIDTH)
        alog = _lane_vec(dn_a_log[l], 2 * DN_HEADS)
        dtb = _lane_vec(dn_dt_bias[l], 2 * DN_HEADS)
        gn = jnp.tile(dn_norm[l], DN_HEADS).reshape(1, DN_VW)

        x = _ffn_call(x, mod_x, 0, norm_ffn1[l], wf1_in, wf1_out)
        xc = _ffn_call(xc, mod_c, 0, norm_ffn1[l], wf1_in, wf1_out)

        px = _proj_call(x, mod_x, 1, norm_mix[l], w_perm, cos_x, sin_x, qg, kg, e_q, p_q, e_k, p_k)
        pc = _proj_call(xc, mod_c, 1, norm_mix[l], w_perm, cos_c, sin_c, qg, kg, e_q, p_q, e_k, p_k)
        pool_x, qkv_x, z_x, ba_x, aq_x, ak_x, av_x, gate_x = px
        pool_c, qkv_c, z_c, ba_c, aq_c, ak_c, av_c, gate_c = [t.reshape(nb, lc, t.shape[-1]) for t in pc]

        yp_x = _pool_call(pool_x, pool_bd, pool_s)
        qd_x, kd_x, vd_x = _conv_call(qkv_x, dn_conv[l], e_sum)
        qd_c, kd_c, vd_c = _conv_call(qkv_c, dn_conv[l], e_sum)
        yd_c, yd_x = _delta_call((qd_c, kd_c, vd_c, ba_c, z_c), (qd_x, kd_x, vd_x, ba_x, z_x),
                                 alog, dtb, gn, e_mean)

        kt_all, v_all = _kv_layout(ak_c, av_c, ak_x, av_x)
        ya_x = _attn_call(aq_x, kt_all, v_all)
        x = _merge_call(x, mod_x, 1, yp_x, yd_x, ya_x, gate_x, w_br, w_o)

        if ctx_out:
            yp_c = _pool_call(pool_c, pool_bd, pool_s)
            kt_c, v_c = _kv_layout(ak_c, av_c)
            ya_c = _attn_call(aq_c, kt_c, v_c)
            flat = lambda t: t.reshape(1, nb * lc, t.shape[-1])
            xc = _merge_call(xc, mod_c, 1, flat(yp_c), flat(yd_c), flat(ya_c), flat(gate_c), w_br, w_o)

        x = _ffn_call(x, mod_x, 2, norm_ffn2[l], wf2_in, wf2_out)
        if ctx_out:
            xc = _ffn_call(xc, mod_c, 2, norm_ffn2[l], wf2_in, wf2_out)

    return _final_norm_call(x, norm_final)
```

```python
import functools
import math

import jax
import jax.numpy as jnp
import numpy as np
from jax import lax
from jax.experimental import pallas as pl
from jax.experimental.pallas import tpu as pltpu

F32 = jnp.float32
BF16 = jnp.bfloat16

N_MOD = 9
FFN_RES = 0.5
POOL_WIDTH = 256
POOL_WINDOWS = (2, 4, 8, 16)
POOL_GROUP = 64
DN_HEADS = 4
DN_DK = 64
DN_DV = 64
DN_CONV = 5
DN_CHUNK = 64
DN_QK = DN_HEADS * DN_DK
DN_VW = DN_HEADS * DN_DV
DN_CONV_CH = 2 * DN_QK + DN_VW
ATT_HEADS = 8
ATT_KV = 2
ATT_GROUP = ATT_HEADS // ATT_KV
ATT_HD = 64
ROPE_BASE = 10000.0
GRID_W = 64
N_BRANCH = 3
EPS = 1e-6
LOG2E = 1.4426950408889634

LANE = 128
SUBLANE = 8
VMEM_LIMIT = 56 * 1024 * 1024

BA_PAD = LANE
Q_W = ATT_HEADS * ATT_HD
KV_W = ATT_KV * ATT_HD
V_ROWS = ATT_HD + 16
C_POOL = 0
C_QKV = C_POOL + POOL_WIDTH
C_Z = C_QKV + DN_CONV_CH
C_AQ = C_Z + DN_VW
C_AK = C_AQ + Q_W
C_AV = C_AK + KV_W
C_GATE = C_AV + KV_W


def _params(*sem):
    return pltpu.CompilerParams(dimension_semantics=sem, vmem_limit_bytes=VMEM_LIMIT)


def _resident(shape):
    nd = len(shape)
    return pl.BlockSpec(shape, lambda *_: (0,) * nd, pipeline_mode=pl.Buffered(1))


def _silu(x):
    return x * jax.nn.sigmoid(x)


def _dot(a, b):
    return jnp.dot(a, b, preferred_element_type=F32)


def _dot_nt(a, b):
    return lax.dot_general(a, b, (((1,), (1,)), ((), ())), preferred_element_type=F32)


def _dot_tn(a, b):
    return lax.dot_general(a, b, (((0,), (0,)), ((), ())), preferred_element_type=F32)


def _split3(x):
    hi = x.astype(BF16)
    r1 = x - hi.astype(F32)
    mid = r1.astype(BF16)
    lo = (r1 - mid.astype(F32)).astype(BF16)
    return hi, mid, lo


def _dot_exact_lhs(a_bf16, x_f32):
    hi, mid, lo = _split3(x_f32)
    return _dot(a_bf16, hi) + _dot(a_bf16, mid) + _dot(a_bf16, lo)


def _ada_kernel(s_ref, w_ref, b_ref, o_ref):
    s = _silu(s_ref[...])
    w = w_ref[0]
    s_hi, s_mid, s_lo = _split3(s)
    w_hi, w_mid, w_lo = _split3(w)
    acc = _dot(s_hi, w_hi)
    acc = acc + _dot(s_hi, w_mid) + _dot(s_mid, w_hi)
    acc = acc + _dot(s_hi, w_lo) + _dot(s_mid, w_mid) + _dot(s_lo, w_hi)
    o_ref[0] = acc + b_ref[0]


def _ada_call(cond, w_ada, b_ada):
    depth, d, n = w_ada.shape
    rows = cond.shape[0]
    tn = 1024
    return pl.pallas_call(
        _ada_kernel,
        out_shape=jax.ShapeDtypeStruct((depth, rows, n), F32),
        grid=(depth, n // tn),
        in_specs=[
            pl.BlockSpec((rows, d), lambda l, j: (0, 0)),
            pl.BlockSpec((1, d, tn), lambda l, j: (l, 0, j)),
            pl.BlockSpec((1, 1, tn), lambda l, j: (l, 0, j)),
        ],
        out_specs=pl.BlockSpec((1, rows, tn), lambda l, j: (l, 0, j)),
        compiler_params=_params("parallel", "parallel"),
        name="ada_mod",
    )(cond, w_ada, b_ada.reshape(depth, 1, n))


def _norm_mod(x, gain, shift, scale):
    ms = jnp.mean(x * x, axis=-1, keepdims=True)
    h = x * lax.rsqrt(ms + EPS) * gain
    return h * (1.0 + scale) + shift


def _ffn_kernel(x_ref, mod_ref, g_ref, wi_ref, wo_ref, o_ref, acc_ref, *, d_ff, tf):
    x = x_ref[0]
    h = _norm_mod(x, g_ref[...], mod_ref[0, 0], mod_ref[0, 1]).astype(BF16)
    for j in range(d_ff // tf):
        g = _dot(h, wi_ref[:, j * tf:(j + 1) * tf])
        u = _dot(h, wi_ref[:, d_ff + j * tf:d_ff + (j + 1) * tf])
        a = (_silu(g) * u).astype(BF16)
        part = _dot(a, wo_ref[j * tf:(j + 1) * tf, :])
        if j == 0:
            acc_ref[...] = part
        else:
            acc_ref[...] += part
    o_ref[0] = x + (FFN_RES * mod_ref[0, 2]) * acc_ref[...]


def _ffn_call(x, mod, sub, gain, w_in, w_out):
    nb, t, d = x.shape
    d_ff = w_out.shape[0]
    tm = min(512, t)
    tf = 256
    kern = functools.partial(_ffn_kernel, d_ff=d_ff, tf=tf)
    return pl.pallas_call(
        kern,
        out_shape=jax.ShapeDtypeStruct(x.shape, F32),
        grid=(nb, t // tm),
        in_specs=[
            pl.BlockSpec((1, tm, d), lambda b, i: (b, i, 0)),
            pl.BlockSpec((1, 3, 1, d), lambda b, i: (b, sub, 0, 0)),
            _resident((1, d)),
            _resident((d, 2 * d_ff)),
            _resident((d_ff, d)),
        ],
        out_specs=pl.BlockSpec((1, tm, d), lambda b, i: (b, i, 0)),
        scratch_shapes=[pltpu.VMEM((tm, d), F32)],
        compiler_params=_params("parallel", "parallel"),
        name="ffn",
    )(x, mod, gain.reshape(1, d), w_in, w_out)


def _head_norm_rope(t_raw, gain, e_mat, p_mat, cos, sin, out_scale):
    ms = _dot((t_raw * t_raw).astype(BF16), e_mat)
    r = lax.rsqrt(ms + EPS) * out_scale
    t = t_raw * gain
    t_sw = _dot(t.astype(BF16), p_mat)
    w = t_raw.shape[1]
    outs = []
    for g in range(w // LANE):
        sl = slice(g * LANE, (g + 1) * LANE)
        outs.append(r[:, sl] * (t[:, sl] * cos + t_sw[:, sl] * sin))
    return jnp.concatenate(outs, axis=1) if len(outs) > 1 else outs[0]


def _proj_kernel(x_ref, mod_ref, g_ref, w_ref, cos_ref, sin_ref, qg_ref, kg_ref,
                 eq_ref, pq_ref, ek_ref, pk_ref,
                 pool_ref, qkv_ref, z_ref, ba_ref, q_ref, k_ref, v_ref, gate_ref, *, n_gate):
    x = x_ref[0]
    h = _norm_mod(x, g_ref[...], mod_ref[0, 0], mod_ref[0, 1]).astype(BF16)

    def cols(lo, width):
        return _dot(h, w_ref[:, lo:lo + width])

    pool_ref[0] = cols(C_POOL, POOL_WIDTH)
    for j in range(DN_CONV_CH // 256):
        qkv_ref[0, :, j * 256:(j + 1) * 256] = cols(C_QKV + j * 256, 256)
    z_ref[0] = cols(C_Z, DN_VW).astype(BF16)
    cos = cos_ref[...]
    sin = sin_ref[...]
    q = _head_norm_rope(cols(C_AQ, Q_W), qg_ref[...], eq_ref[...], pq_ref[...], cos, sin,
                        (ATT_HD ** -0.5) * LOG2E)
    q_ref[0] = q.astype(BF16)
    k = _head_norm_rope(cols(C_AK, KV_W), kg_ref[...], ek_ref[...], pk_ref[...], cos, sin, 1.0)
    k_ref[0] = k.astype(BF16)
    v_ref[0] = cols(C_AV, KV_W).astype(BF16)
    tg = 512
    for j in range(n_gate // tg):
        gate_ref[0, :, j * tg:(j + 1) * tg] = jax.nn.sigmoid(cols(C_GATE + j * tg, tg)).astype(BF16)
    ba_ref[0] = cols(C_GATE + n_gate, BA_PAD)


def _proj_call(x, mod, sub, gain, w_perm, cos, sin, qg, kg, eq, pq, ek, pk):
    nb, t, d = x.shape
    n_gate = N_BRANCH * d
    ncol = w_perm.shape[1]
    tm = min(512, t)
    nt = t // tm
    row = lambda w: pl.BlockSpec((1, tm, w), lambda b, i: (b, i, 0))
    out_shapes = [
        jax.ShapeDtypeStruct((nb, t, POOL_WIDTH), F32),
        jax.ShapeDtypeStruct((nb, t, DN_CONV_CH), F32),
        jax.ShapeDtypeStruct((nb, t, DN_VW), BF16),
        jax.ShapeDtypeStruct((nb, t, BA_PAD), F32),
        jax.ShapeDtypeStruct((nb, t, Q_W), BF16),
        jax.ShapeDtypeStruct((nb, t, KV_W), BF16),
        jax.ShapeDtypeStruct((nb, t, KV_W), BF16),
        jax.ShapeDtypeStruct((nb, t, n_gate), BF16),
    ]
    out_specs = [row(POOL_WIDTH), row(DN_CONV_CH), row(DN_VW), row(BA_PAD), row(Q_W), row(KV_W),
                 row(KV_W), row(n_gate)]
    kern = functools.partial(_proj_kernel, n_gate=n_gate)
    return pl.pallas_call(
        kern,
        out_shape=out_shapes,
        grid=(nb, nt),
        in_specs=[
            pl.BlockSpec((1, tm, d), lambda b, i: (b, i, 0)),
            pl.BlockSpec((1, 3, 1, d), lambda b, i: (b, sub, 0, 0)),
            _resident((1, d)),
            _resident((d, ncol)),
            pl.BlockSpec((tm, LANE), lambda b, i: (i, 0)),
            pl.BlockSpec((tm, LANE), lambda b, i: (i, 0)),
            _resident((1, Q_W)),
            _resident((1, KV_W)),
            _resident((Q_W, Q_W)),
            _resident((Q_W, Q_W)),
            _resident((KV_W, KV_W)),
            _resident((KV_W, KV_W)),
        ],
        out_specs=out_specs,
        compiler_params=_params("parallel", "parallel"),
        name="mix_proj",
    )(x, mod, gain.reshape(1, d), w_perm, cos, sin, qg, kg, eq, pq, ek, pk)


def _fill_halo(pad_ref, prev_ref, cur_ref, next_ref, tt):
    i = pl.program_id(1)
    n = pl.num_programs(1)
    prev = prev_ref[0]
    nxt = next_ref[0]
    pad_ref[0:SUBLANE, :] = jnp.where(i > 0, prev, jnp.zeros_like(prev))
    pad_ref[SUBLANE:SUBLANE + tt, :] = cur_ref[0]
    pad_ref[SUBLANE + tt:2 * SUBLANE + tt, :] = jnp.where(i < n - 1, nxt, jnp.zeros_like(nxt))


def _halo_specs(tt, width, n_rows):
    per = tt // SUBLANE
    last = n_rows // SUBLANE - 1
    return [
        pl.BlockSpec((1, SUBLANE, width), lambda b, i: (b, jnp.maximum(i * per - 1, 0), 0)),
        pl.BlockSpec((1, tt, width), lambda b, i: (b, i, 0)),
        pl.BlockSpec((1, SUBLANE, width), lambda b, i: (b, jnp.minimum((i + 1) * per, last), 0)),
    ]


def _pool_kernel(prev_ref, cur_ref, next_ref, w_ref, s_ref, o_ref, pad_ref, *, tt, seq):
    _fill_halo(pad_ref, prev_ref, cur_ref, next_ref, tt)
    i = pl.program_id(1)
    t = i * tt + lax.broadcasted_iota(jnp.int32, (tt, 1), 0)
    lane = lax.broadcasted_iota(jnp.int32, (1, POOL_WIDTH), 1)
    cur = pad_ref[SUBLANE:SUBLANE + tt, :]
    y = jnp.zeros((tt, POOL_WIDTH), F32)
    for gi, w in enumerate(POOL_WINDOWS):
        acc = None
        for off in range(-(w // 2), w - w // 2):
            term = pad_ref[SUBLANE + off:SUBLANE + off + tt, :]
            acc = term if acc is None else acc + term
        lo = jnp.maximum(t - w // 2, 0)
        hi = jnp.minimum(t + w - w // 2, seq)
        cnt = (hi - lo).astype(F32)
        in_group = (lane >= gi * POOL_GROUP) & (lane < (gi + 1) * POOL_GROUP)
        y = jnp.where(in_group, acc / cnt - cur, y)
    o_ref[0] = (_dot(y.astype(BF16), w_ref[...]) * s_ref[...]).astype(BF16)


def _pool_call(p, w_bd, scale):
    nb, t, c = p.shape
    tt = min(512, t)
    kern = functools.partial(_pool_kernel, tt=tt, seq=t)
    return pl.pallas_call(
        kern,
        out_shape=jax.ShapeDtypeStruct((nb, t, c), BF16),
        grid=(nb, t // tt),
        in_specs=_halo_specs(tt, c, t) + [_resident((c, c)), _resident((1, c))],
        out_specs=pl.BlockSpec((1, tt, c), lambda b, i: (b, i, 0)),
        scratch_shapes=[pltpu.VMEM((tt + 2 * SUBLANE, c), F32)],
        compiler_params=_params("parallel", "arbitrary"),
        name="pool_mix",
    )(p, p, p, w_bd, scale)


def _conv_kernel(prev_ref, cur_ref, next_ref, w_ref, e_ref, q_ref, k_ref, v_ref, pad_ref, *, tt):
    _fill_halo(pad_ref, prev_ref, cur_ref, next_ref, tt)
    half = DN_CONV // 2
    outs = (q_ref, k_ref, v_ref)
    for part in range(3):
        cs = slice(part * DN_QK, (part + 1) * DN_QK)
        acc = None
        for j in range(DN_CONV):
            lo = SUBLANE + j - half
            term = pad_ref[lo:lo + tt, cs] * w_ref[j:j + 1, cs]
            acc = term if acc is None else acc + term
        y = _silu(acc)
        if part < 2:
            ss = _dot((y * y).astype(BF16), e_ref[...])
            y = y * lax.rsqrt(ss + EPS)
            if part == 0:
                y = y * (DN_DK ** -0.5)
        outs[part][0] = y.astype(BF16)


def _conv_call(qkv, conv_w, e_sum):
    nb, t, c = qkv.shape
    tt = min(512, t)
    kern = functools.partial(_conv_kernel, tt=tt)
    o = jax.ShapeDtypeStruct((nb, t, DN_QK), BF16)
    ospec = pl.BlockSpec((1, tt, DN_QK), lambda b, i: (b, i, 0))
    return pl.pallas_call(
        kern,
        out_shape=[o, o, o],
        grid=(nb, t // tt),
        in_specs=_halo_specs(tt, c, t) + [_resident((DN_CONV, c)), _resident((DN_QK, DN_QK))],
        out_specs=[ospec, ospec, ospec],
        scratch_shapes=[pltpu.VMEM((tt + 2 * SUBLANE, c), F32)],
        compiler_params=_params("parallel", "arbitrary"),
        name="dn_conv",
    )(qkv, qkv, qkv, conv_w, e_sum)


def _delta_kernel(qc_ref, kc_ref, vc_ref, bac_ref, zc_ref, qx_ref, kx_ref, vx_ref, bax_ref, zx_ref,
                  alog_ref, dtb_ref, gn_ref, e_ref, yc_ref, yx_ref,
                  sf_ref, sb_ref, oc_ref, ox_ref, *, n_ctx, n_lat):
    c = DN_CHUNK
    w = DN_HEADS * c
    row = lax.broadcasted_iota(jnp.int32, (w, w), 0)
    col = lax.broadcasted_iota(jnp.int32, (w, w), 1)
    rc_xor = row ^ col
    same = (rc_xor >> 6) == 0
    rp = row & (c - 1)
    cp = col & (c - 1)
    eye = jnp.where(row == col, 1.0, 0.0).astype(F32)
    m_incl = (same & (rp >= cp), same & (rp <= cp))
    m_strict = (same & (rp > cp), same & (rp < cp))
    ones_bd = jnp.where(same, 1.0, 0.0).astype(BF16)
    m_incl_b = tuple(jnp.where(m, 1.0, 0.0).astype(BF16) for m in m_incl)
    m_strict_f = tuple(jnp.where(m, 1.0, 0.0).astype(F32) for m in m_strict)
    lane = lax.broadcasted_iota(jnp.int32, (w, LANE), 1)
    rhead = lax.broadcasted_iota(jnp.int32, (w, LANE), 0) // c
    neg_a = -jnp.exp(alog_ref[...])
    dtb = dtb_ref[...]

    sf_ref[...] = jnp.zeros_like(sf_ref)
    sb_ref[...] = jnp.zeros_like(sb_ref)
    oc_ref[...] = jnp.zeros_like(oc_ref)
    ox_ref[...] = jnp.zeros_like(ox_ref)

    def tile4(a):
        return jnp.concatenate([a, a, a, a], axis=0)

    def one_direction(d, q_bd, k_bd, v_bd, kk, qk, ba4, s_ref):
        beta_all = jax.nn.sigmoid(ba4)
        g_all = neg_a * jax.nn.softplus(ba4 + dtb)
        beta = jnp.sum(jnp.where(lane == d * DN_HEADS + rhead, beta_all, 0.0), axis=1, keepdims=True)
        g = jnp.sum(jnp.where(lane == 2 * DN_HEADS + d * DN_HEADS + rhead, g_all, 0.0), axis=1,
                    keepdims=True)
        g_wide = jnp.broadcast_to(g, (w, LANE))
        g_incl = _dot_exact_lhs(m_incl_b[d], g_wide)[:, :1]
        g_tot = _dot_exact_lhs(ones_bd, g_wide)[:, :1]
        diff = _dot_exact_lhs(m_incl_b[d], g * m_strict_f[d])
        decay = jnp.where(m_incl[d], jnp.exp(diff), 0.0)
        e_g = jnp.exp(g_incl)
        e_rest = jnp.exp(g_tot - g_incl)
        e_tot = jnp.exp(g_tot)
        a = jnp.where(m_strict[d], beta * kk * decay, 0.0)
        t = eye - jnp.where((rc_xor >> 1) == 0, a, 0.0)
        for lg in range(1, 6):
            b_ring = jnp.where((rc_xor >> lg) == 1, a, 0.0).astype(BF16)
            t_b = t.astype(BF16)
            t = t - _dot(_dot(t_b, b_ring).astype(BF16), t_b)
        t_b = t.astype(BF16)
        u = _dot(t_b, (v_bd * beta).astype(BF16))
        wm = _dot(t_b, (k_bd * (beta * e_g)).astype(BF16))
        s_old = s_ref[...]
        s_b = s_old.astype(BF16)
        v_new = u - _dot(wm.astype(BF16), s_b)
        v_new_b = v_new.astype(BF16)
        o = _dot((q_bd * e_g).astype(BF16), s_b) + _dot((qk * decay).astype(BF16), v_new_b)
        s_ref[...] = s_old * e_tot + _dot_tn((k_bd * e_rest).astype(BF16), v_new_b)
        return o[0:c] + o[c:2 * c] + o[2 * c:3 * c] + o[3 * c:4 * c]

    def chunk_step(refs, o_ref, rows_f, rows_b):
        q_ref, k_ref, v_ref, ba_ref = refs
        for d, rows in ((0, rows_f), (1, rows_b)):
            q_bd = jnp.where(same, tile4(q_ref[0, rows, :].astype(F32)), 0.0)
            k_bd = jnp.where(same, tile4(k_ref[0, rows, :].astype(F32)), 0.0)
            v_bd = jnp.where(same, tile4(v_ref[0, rows, :].astype(F32)), 0.0)
            k_b = k_bd.astype(BF16)
            kk = _dot_nt(k_b, k_b)
            qk = _dot_nt(q_bd.astype(BF16), k_b)
            ba4 = tile4(ba_ref[0, rows, :])
            o_tok = one_direction(d, q_bd, k_bd, v_bd, kk, qk, ba4, sf_ref if d == 0 else sb_ref)
            o_ref[rows, :] += o_tok

    ctx_refs = (qc_ref, kc_ref, vc_ref, bac_ref)
    lat_refs = (qx_ref, kx_ref, vx_ref, bax_ref)
    for i in range(n_ctx):
        chunk_step(ctx_refs, oc_ref, pl.ds(i * c, c), pl.ds((n_ctx - 1 - i) * c, c))

    def lat_body(i, carry):
        f0 = pl.multiple_of(i * c, c)
        b0 = pl.multiple_of((n_lat - 1 - i) * c, c)
        chunk_step(lat_refs, ox_ref, pl.ds(f0, c), pl.ds(b0, c))
        return carry

    lax.fori_loop(0, n_lat, lat_body, 0)

    def finish(o_ref, z_ref, y_ref, n_rows):
        tt = min(512, n_rows)
        for j in range(n_rows // tt):
            rs = slice(j * tt, (j + 1) * tt)
            o = o_ref[rs, :]
            ms = _dot((o * o).astype(BF16), e_ref[...])
            y = o * lax.rsqrt(ms + EPS) * gn_ref[...]
            y_ref[0, rs, :] = (y * _silu(z_ref[0, rs, :].astype(F32))).astype(BF16)

    finish(oc_ref, zc_ref, yc_ref, n_ctx * c)
    finish(ox_ref, zx_ref, yx_ref, n_lat * c)


def _delta_call(ctx_parts, lat_parts, alog, dtb, gn, e_mean):
    qc, kc, vc, bac, zc = ctx_parts
    qx, kx, vx, bax, zx = lat_parts
    nb, lc, wd = qc.shape
    lx = qx.shape[1]
    kern = functools.partial(_delta_kernel, n_ctx=lc // DN_CHUNK, n_lat=lx // DN_CHUNK)
    seq = lambda n, width: pl.BlockSpec((1, n, width), lambda b: (b, 0, 0))
    in_specs = [seq(lc, wd), seq(lc, wd), seq(lc, wd), seq(lc, BA_PAD), seq(lc, wd),
                seq(lx, wd), seq(lx, wd), seq(lx, wd), seq(lx, BA_PAD), seq(lx, wd),
                _resident((1, LANE)), _resident((1, LANE)), _resident((1, wd)), _resident((wd, wd))]
    return pl.pallas_call(
        kern,
        out_shape=[jax.ShapeDtypeStruct((nb, lc, wd), BF16), jax.ShapeDtypeStruct((nb, lx, wd), BF16)],
        grid=(nb,),
        in_specs=in_specs,
        out_specs=[seq(lc, wd), seq(lx, wd)],
        scratch_shapes=[pltpu.VMEM((wd, wd), F32), pltpu.VMEM((wd, wd), F32),
                        pltpu.VMEM((lc, wd), F32), pltpu.VMEM((lx, wd), F32)],
        compiler_params=_params("parallel"),
        name="delta_rule",
    )(qc, kc, vc, bac, zc, qx, kx, vx, bax, zx, alog, dtb, gn, e_mean)


def _attn_kernel(q_ref, k_ref, vt_ref, o_ref, qs_ref, s_ref, p_ref, al_ref, m_ref, acc_ref, *, tq, tk, n_k):
    for h in range(ATT_GROUP):
        qs_ref[h * tq:(h + 1) * tq, :] = q_ref[0, :, h * ATT_HD:(h + 1) * ATT_HD]
    m_ref[...] = jnp.full(m_ref.shape, -1e30, F32)
    acc_ref[...] = jnp.zeros(acc_ref.shape, F32)

    def scores(j, slot):
        s_ref[slot] = _dot_nt(k_ref[0, 0, j * tk:(j + 1) * tk, :], qs_ref[...])

    def softmax(slot):
        s = s_ref[slot]
        m_old = m_ref[...]
        m_new = jnp.maximum(m_old, jnp.max(s, axis=0, keepdims=True))
        p_ref[slot] = jnp.exp2(s - m_new).astype(BF16)
        al_ref[slot] = jnp.exp2(m_old - m_new)
        m_ref[...] = m_new

    def values(j, slot):
        acc_ref[...] = acc_ref[...] * al_ref[slot] + _dot(vt_ref[0, 0, :, j * tk:(j + 1) * tk], p_ref[slot])

    for i in range(n_k + 2):
        if i < n_k:
            scores(i, i % 2)
        if 1 <= i <= n_k:
            softmax((i - 1) % 2)
        if i >= 2:
            values(i - 2, i % 2)

    acc = acc_ref[...]
    out = acc[:ATT_HD] / acc[ATT_HD:ATT_HD + 1]
    o_ref[0] = jnp.concatenate([out[:, h * tq:(h + 1) * tq].T for h in range(ATT_GROUP)], axis=1).astype(BF16)


def _attn_call(q, k4, vt):
    nb, lq, _ = q.shape
    s = k4.shape[2]
    tq = min(256, lq)
    tk = 256
    gw = ATT_GROUP * ATT_HD
    cols = ATT_GROUP * tq
    kern = functools.partial(_attn_kernel, tq=tq, tk=tk, n_k=s // tk)
    return pl.pallas_call(
        kern,
        out_shape=jax.ShapeDtypeStruct((nb, lq, Q_W), BF16),
        grid=(nb, ATT_KV, lq // tq),
        in_specs=[
            pl.BlockSpec((1, tq, gw), lambda b, g, i: (b, i, g)),
            pl.BlockSpec((1, 1, s, ATT_HD), lambda b, g, i: (b, g, 0, 0)),
            pl.BlockSpec((1, 1, V_ROWS, s), lambda b, g, i: (b, g, 0, 0)),
        ],
        out_specs=pl.BlockSpec((1, tq, gw), lambda b, g, i: (b, i, g)),
        scratch_shapes=[pltpu.VMEM((cols, ATT_HD), BF16), pltpu.VMEM((2, tk, cols), F32),
                        pltpu.VMEM((2, tk, cols), BF16), pltpu.VMEM((2, 1, cols), F32),
                        pltpu.VMEM((1, cols), F32), pltpu.VMEM((V_ROWS, cols), F32)],
        compiler_params=_params("parallel", "parallel", "arbitrary"),
        name="gqa_attn",
    )(q, k4, vt)


def _merge_kernel(x_ref, mod_ref, yp_ref, yd_ref, ya_ref, gate_ref, wb_ref, wo_ref, o_ref):
    d = x_ref.shape[2]
    a0 = POOL_WIDTH
    a1 = POOL_WIDTH + DN_VW
    m = gate_ref[0, :, 0:d].astype(F32) * _dot(yp_ref[0], wb_ref[0:a0, :])
    m = m + gate_ref[0, :, d:2 * d].astype(F32) * _dot(yd_ref[0], wb_ref[a0:a1, :])
    m = m + gate_ref[0, :, 2 * d:3 * d].astype(F32) * _dot(ya_ref[0], wb_ref[a1:, :])
    o_ref[0] = x_ref[0] + mod_ref[0, 2] * _dot(m.astype(BF16), wo_ref[...])


def _merge_call(x, mod, sub, y_pool, y_dn, y_att, gate, w_branch, w_out):
    nb, t, d = x.shape
    tm = min(512, t)
    row = lambda w: pl.BlockSpec((1, tm, w), lambda b, i: (b, i, 0))
    return pl.pallas_call(
        _merge_kernel,
        out_shape=jax.ShapeDtypeStruct(x.shape, F32),
        grid=(nb, t // tm),
        in_specs=[row(d), pl.BlockSpec((1, 3, 1, d), lambda b, i: (b, sub, 0, 0)),
                  row(POOL_WIDTH), row(DN_VW), row(Q_W), row(N_BRANCH * d),
                  _resident(w_branch.shape), _resident(w_out.shape)],
        out_specs=row(d),
        compiler_params=_params("parallel", "parallel"),
        name="merge",
    )(x, mod, y_pool, y_dn, y_att, gate, w_branch, w_out)


def _final_norm_kernel(x_ref, g_ref, o_ref):
    x = x_ref[0]
    ms = jnp.mean(x * x, axis=-1, keepdims=True)
    o_ref[0] = x * lax.rsqrt(ms + EPS) * g_ref[...]


def _final_norm_call(x, gain):
    nb, t, d = x.shape
    tm = min(1024, t)
    return pl.pallas_call(
        _final_norm_kernel,
        out_shape=jax.ShapeDtypeStruct(x.shape, F32),
        grid=(nb, t // tm),
        in_specs=[pl.BlockSpec((1, tm, d), lambda b, i: (b, i, 0)), _resident((1, d))],
        out_specs=pl.BlockSpec((1, tm, d), lambda b, i: (b, i, 0)),
        compiler_params=_params("parallel", "parallel"),
        name="final_norm",
    )(x, gain.reshape(1, d))


def _block_diag_const(width, block, value):
    idx = np.arange(width)
    return jnp.asarray(np.where((idx[:, None] // block) == (idx[None, :] // block), value, 0.0), BF16)


def _pair_swap_const(width):
    idx = np.arange(width)
    return jnp.asarray((idx[:, None] == (idx[None, :] ^ 1)).astype(np.float32), BF16)


def _rope_tables(n_tokens):
    rows = n_tokens // GRID_W
    row = jnp.broadcast_to(jnp.arange(rows)[:, None], (rows, GRID_W)).reshape(-1)
    col = jnp.broadcast_to(jnp.arange(GRID_W)[None, :], (rows, GRID_W)).reshape(-1)
    n_freq = ATT_HD // 4
    inv = ROPE_BASE ** (-jnp.arange(n_freq, dtype=F32) / n_freq)
    ang = jnp.concatenate([row[:, None].astype(F32) * inv, col[:, None].astype(F32) * inv], axis=-1)
    cos = jnp.repeat(jnp.cos(ang), 2, axis=1)
    sin = jnp.repeat(jnp.sin(ang), 2, axis=1)
    sign = jnp.where(jnp.arange(ATT_HD) % 2 == 0, -1.0, 1.0).astype(F32)
    sin = sin * sign
    reps = LANE // ATT_HD
    return jnp.tile(cos, (1, reps)), jnp.tile(sin, (1, reps))


def _permute_w_in(w_in):
    d = w_in.shape[0]
    o_pool = 0
    o_qkv = o_pool + POOL_WIDTH
    o_z = o_qkv + DN_CONV_CH
    o_ba = o_z + DN_VW
    o_aq = o_ba + 4 * DN_HEADS
    o_gate = o_aq + Q_W + 2 * KV_W
    pad = jnp.zeros((d, BA_PAD - 4 * DN_HEADS), w_in.dtype)
    return jnp.concatenate([w_in[:, o_pool:o_ba], w_in[:, o_aq:], w_in[:, o_ba:o_aq], pad], axis=1).astype(BF16)


def _lane_vec(vals, offset):
    out = jnp.zeros((1, LANE), F32)
    return out.at[0, offset:offset + vals.size].set(vals.reshape(-1).astype(F32))


def _kv_layout(k_c, v_c, k_x=None, v_x=None):
    k = k_c if k_x is None else jnp.concatenate([k_c, k_x], axis=1)
    v = v_c if v_x is None else jnp.concatenate([v_c, v_x], axis=1)
    nb, s, _ = k.shape
    k4 = k.reshape(nb, s, ATT_KV, ATT_HD).transpose(0, 2, 1, 3)
    vt = v.reshape(nb, s, ATT_KV, ATT_HD).transpose(0, 2, 3, 1)
    ones = jnp.ones((nb, ATT_KV, 1, s), vt.dtype)
    zeros = jnp.zeros((nb, ATT_KV, V_ROWS - ATT_HD - 1, s), vt.dtype)
    return k4, jnp.concatenate([vt, ones, zeros], axis=2)


def kernel(x, c, ctx, c_ctx, w_ada, b_ada, norm_ffn1, w_ffn1_in, w_ffn1_out, norm_mix, w_in, pool_w, pool_scale, dn_conv, dn_a_log, dn_dt_bias, dn_norm, q_norm, k_norm, w_branch, w_out, norm_ffn2, w_ffn2_in, w_ffn2_out, norm_final):
    nb, seq, d = x.shape
    lc = ctx.shape[1]
    depth = w_ada.shape[0]

    n_rows = -(-(nb + 1) // SUBLANE) * SUBLANE
    cond = jnp.concatenate([c, c_ctx[None], jnp.zeros((n_rows - nb - 1, d), F32)], axis=0)
    mod_all = _ada_call(cond, w_ada, b_ada).reshape(depth, n_rows, N_MOD, 1, d)

    cos_x, sin_x = _rope_tables(seq)
    cos_c = jnp.ones((nb * lc, LANE), F32)
    sin_c = jnp.zeros((nb * lc, LANE), F32)
    e_q = _block_diag_const(Q_W, ATT_HD, 1.0 / ATT_HD)
    e_k = _block_diag_const(KV_W, ATT_HD, 1.0 / ATT_HD)
    p_q = _pair_swap_const(Q_W)
    p_k = _pair_swap_const(KV_W)
    e_sum = _block_diag_const(DN_QK, DN_DK, 1.0)
    e_mean = _block_diag_const(DN_VW, DN_DV, 1.0 / DN_DV)

    xc = ctx.reshape(1, nb * lc, d)
    for l in range(depth):
        ctx_out = l < depth - 1
        mod_x = mod_all[l, :nb]
        mod_c = mod_all[l, nb:nb + 1]
        wf1_in = w_ffn1_in[l].astype(BF16)
        wf1_out = w_ffn1_out[l].astype(BF16)
        wf2_in = w_ffn2_in[l].astype(BF16)
        wf2_out = w_ffn2_out[l].astype(BF16)
        w_perm = _permute_w_in(w_in[l])
        w_br = w_branch[l].astype(BF16)
        w_o = w_out[l].astype(BF16)
        qg = jnp.tile(q_norm[l], ATT_HEADS).reshape(1, Q_W)
        kg = jnp.tile(k_norm[l], ATT_KV).reshape(1, KV_W)
        pool_bd = jax.scipy.linalg.block_diag(*[pool_w[l, g] for g in range(len(POOL_WINDOWS))]).astype(BF16)
        pool_s = pool_scale[l].reshape(1, POOL_WIDTH)
        alog = _lane_vec(dn_a_log[l], 2 * DN_HEADS)
        dtb = _lane_vec(dn_dt_bias[l], 2 * DN_HEADS)
        gn = jnp.tile(dn_norm[l], DN_HEADS).reshape(1, DN_VW)

        x = _ffn_call(x, mod_x, 0, norm_ffn1[l], wf1_in, wf1_out)
        xc = _ffn_call(xc, mod_c, 0, norm_ffn1[l], wf1_in, wf1_out)

        px = _proj_call(x, mod_x, 1, norm_mix[l], w_perm, cos_x, sin_x, qg, kg, e_q, p_q, e_k, p_k)
        pc = _proj_call(xc, mod_c, 1, norm_mix[l], w_perm, cos_c, sin_c, qg, kg, e_q, p_q, e_k, p_k)
        pool_x, qkv_x, z_x, ba_x, aq_x, ak_x, av_x, gate_x = px
        pool_c, qkv_c, z_c, ba_c, aq_c, ak_c, av_c, gate_c = [t.reshape(nb, lc, t.shape[-1]) for t in pc]

        yp_x = _pool_call(pool_x, pool_bd, pool_s)
        qd_x, kd_x, vd_x = _conv_call(qkv_x, dn_conv[l], e_sum)
        qd_c, kd_c, vd_c = _conv_call(qkv_c, dn_conv[l], e_sum)
        yd_c, yd_x = _delta_call((qd_c, kd_c, vd_c, ba_c, z_c), (qd_x, kd_x, vd_x, ba_x, z_x),
                                 alog, dtb, gn, e_mean)

        k4_all, vt_all = _kv_layout(ak_c, av_c, ak_x, av_x)
        ya_x = _attn_call(aq_x, k4_all, vt_all)
        x = _merge_call(x, mod_x, 1, yp_x, yd_x, ya_x, gate_x, w_br, w_o)

        if ctx_out:
            yp_c = _pool_call(pool_c, pool_bd, pool_s)
            k4_c, vt_c = _kv_layout(ak_c, av_c)
            ya_c = _attn_call(aq_c, k4_c, vt_c)
            flat = lambda t: t.reshape(1, nb * lc, t.shape[-1])
            xc = _merge_call(xc, mod_c, 1, flat(yp_c), flat(yd_c), flat(ya_c), flat(gate_c), w_br, w_o)

        x = _ffn_call(x, mod_x, 2, norm_ffn2[l], wf2_in, wf2_out)
        if ctx_out:
            xc = _ffn_call(xc, mod_c, 2, norm_ffn2[l], wf2_in, wf2_out)

    return _final_norm_call(x, norm_final)
```

```python
import functools
import math

import jax
import jax.numpy as jnp
import numpy as np
from jax import lax
from jax.experimental import pallas as pl
from jax.experimental.pallas import tpu as pltpu

F32 = jnp.float32
BF16 = jnp.bfloat16

N_MOD = 9
FFN_RES = 0.5
POOL_WIDTH = 256
POOL_WINDOWS = (2, 4, 8, 16)
POOL_GROUP = 64
DN_HEADS = 4
DN_DK = 64
DN_DV = 64
DN_CONV = 5
DN_CHUNK = 64
DN_QK = DN_HEADS * DN_DK
DN_VW = DN_HEADS * DN_DV
DN_CONV_CH = 2 * DN_QK + DN_VW
ATT_HEADS = 8
ATT_KV = 2
ATT_GROUP = ATT_HEADS // ATT_KV
ATT_HD = 64
ROPE_BASE = 10000.0
GRID_W = 64
N_BRANCH = 3
EPS = 1e-6
LOG2E = 1.4426950408889634

LANE = 128
SUBLANE = 8
VMEM_LIMIT = 56 * 1024 * 1024

BA_PAD = LANE
Q_W = ATT_HEADS * ATT_HD
KV_W = ATT_KV * ATT_HD
V_ROWS = ATT_HD + 16
C_POOL = 0
C_QKV = C_POOL + POOL_WIDTH
C_Z = C_QKV + DN_CONV_CH
C_AQ = C_Z + DN_VW
C_AK = C_AQ + Q_W
C_AV = C_AK + KV_W
C_GATE = C_AV + KV_W


def _params(*sem):
    return pltpu.CompilerParams(dimension_semantics=sem, vmem_limit_bytes=VMEM_LIMIT)


def _resident(shape):
    nd = len(shape)
    return pl.BlockSpec(shape, lambda *_: (0,) * nd, pipeline_mode=pl.Buffered(1))


def _silu(x):
    return x * jax.nn.sigmoid(x)


def _dot(a, b):
    return jnp.dot(a, b, preferred_element_type=F32)


def _dot_nt(a, b):
    return lax.dot_general(a, b, (((1,), (1,)), ((), ())), preferred_element_type=F32)


def _dot_tn(a, b):
    return lax.dot_general(a, b, (((0,), (0,)), ((), ())), preferred_element_type=F32)


def _split3(x):
    hi = x.astype(BF16)
    r1 = x - hi.astype(F32)
    mid = r1.astype(BF16)
    lo = (r1 - mid.astype(F32)).astype(BF16)
    return hi, mid, lo


def _dot_exact_lhs(a_bf16, x_f32):
    hi, mid, lo = _split3(x_f32)
    return _dot(a_bf16, hi) + _dot(a_bf16, mid) + _dot(a_bf16, lo)


def _dot_exact_rhs(x_f32, b_bf16):
    hi, mid, lo = _split3(x_f32)
    return _dot(hi, b_bf16) + _dot(mid, b_bf16) + _dot(lo, b_bf16)


def _ada_kernel(s_ref, w_ref, b_ref, o_ref):
    s = _silu(s_ref[...])
    w = w_ref[0]
    s_hi, s_mid, s_lo = _split3(s)
    w_hi, w_mid, w_lo = _split3(w)
    acc = _dot(s_hi, w_hi)
    acc = acc + _dot(s_hi, w_mid) + _dot(s_mid, w_hi)
    acc = acc + _dot(s_hi, w_lo) + _dot(s_mid, w_mid) + _dot(s_lo, w_hi)
    o_ref[0] = acc + b_ref[0]


def _ada_call(cond, w_ada, b_ada):
    depth, d, n = w_ada.shape
    rows = cond.shape[0]
    tn = 1024
    return pl.pallas_call(
        _ada_kernel,
        out_shape=jax.ShapeDtypeStruct((depth, rows, n), F32),
        grid=(depth, n // tn),
        in_specs=[
            pl.BlockSpec((rows, d), lambda l, j: (0, 0)),
            pl.BlockSpec((1, d, tn), lambda l, j: (l, 0, j)),
            pl.BlockSpec((1, 1, tn), lambda l, j: (l, 0, j)),
        ],
        out_specs=pl.BlockSpec((1, rows, tn), lambda l, j: (l, 0, j)),
        compiler_params=_params("parallel", "parallel"),
        name="ada_mod",
    )(cond, w_ada, b_ada.reshape(depth, 1, n))


def _norm_mod(x, gain, shift, scale):
    ms = jnp.mean(x * x, axis=-1, keepdims=True)
    h = x * lax.rsqrt(ms + EPS) * gain
    return h * (1.0 + scale) + shift


def _ffn_kernel(x_ref, mod_ref, g_ref, wi_ref, wo_ref, o_ref, acc_ref, *, d_ff, tf):
    x = x_ref[0]
    h = _norm_mod(x, g_ref[...], mod_ref[0, 0], mod_ref[0, 1]).astype(BF16)
    for j in range(d_ff // tf):
        g = _dot(h, wi_ref[:, j * tf:(j + 1) * tf])
        u = _dot(h, wi_ref[:, d_ff + j * tf:d_ff + (j + 1) * tf])
        a = (_silu(g) * u).astype(BF16)
        part = _dot(a, wo_ref[j * tf:(j + 1) * tf, :])
        if j == 0:
            acc_ref[...] = part
        else:
            acc_ref[...] += part
    o_ref[0] = x + (FFN_RES * mod_ref[0, 2]) * acc_ref[...]


def _ffn_call(x, mod, sub, gain, w_in, w_out):
    nb, t, d = x.shape
    d_ff = w_out.shape[0]
    tm = min(512, t)
    tf = 256
    kern = functools.partial(_ffn_kernel, d_ff=d_ff, tf=tf)
    return pl.pallas_call(
        kern,
        out_shape=jax.ShapeDtypeStruct(x.shape, F32),
        grid=(nb, t // tm),
        in_specs=[
            pl.BlockSpec((1, tm, d), lambda b, i: (b, i, 0)),
            pl.BlockSpec((1, 3, 1, d), lambda b, i: (b, sub, 0, 0)),
            _resident((1, d)),
            _resident((d, 2 * d_ff)),
            _resident((d_ff, d)),
        ],
        out_specs=pl.BlockSpec((1, tm, d), lambda b, i: (b, i, 0)),
        scratch_shapes=[pltpu.VMEM((tm, d), F32)],
        compiler_params=_params("parallel", "parallel"),
        name="ffn",
    )(x, mod, gain.reshape(1, d), w_in, w_out)


def _head_norm_rope(t_raw, gain, e_mat, p_mat, cos, sin, out_scale):
    ms = _dot((t_raw * t_raw).astype(BF16), e_mat)
    r = lax.rsqrt(ms + EPS) * out_scale
    t = t_raw * gain
    t_sw = _dot(t.astype(BF16), p_mat)
    w = t_raw.shape[1]
    outs = []
    for g in range(w // LANE):
        sl = slice(g * LANE, (g + 1) * LANE)
        outs.append(r[:, sl] * (t[:, sl] * cos + t_sw[:, sl] * sin))
    return jnp.concatenate(outs, axis=1) if len(outs) > 1 else outs[0]


def _proj_kernel(x_ref, mod_ref, g_ref, w_ref, cos_ref, sin_ref, qg_ref, kg_ref,
                 eq_ref, pq_ref, ek_ref, pk_ref,
                 pool_ref, qkv_ref, z_ref, ba_ref, q_ref, k_ref, v_ref, gate_ref, *, n_gate):
    x = x_ref[0]
    h = _norm_mod(x, g_ref[...], mod_ref[0, 0], mod_ref[0, 1]).astype(BF16)

    def cols(lo, width):
        return _dot(h, w_ref[:, lo:lo + width])

    pool_ref[0] = cols(C_POOL, POOL_WIDTH)
    for j in range(DN_CONV_CH // 256):
        qkv_ref[0, :, j * 256:(j + 1) * 256] = cols(C_QKV + j * 256, 256)
    z_ref[0] = cols(C_Z, DN_VW).astype(BF16)
    cos = cos_ref[...]
    sin = sin_ref[...]
    q = _head_norm_rope(cols(C_AQ, Q_W), qg_ref[...], eq_ref[...], pq_ref[...], cos, sin,
                        (ATT_HD ** -0.5) * LOG2E)
    q_ref[0] = q.astype(BF16)
    k = _head_norm_rope(cols(C_AK, KV_W), kg_ref[...], ek_ref[...], pk_ref[...], cos, sin, 1.0)
    k_ref[0] = k.astype(BF16)
    v_ref[0] = cols(C_AV, KV_W).astype(BF16)
    tg = 512
    for j in range(n_gate // tg):
        gate_ref[0, :, j * tg:(j + 1) * tg] = jax.nn.sigmoid(cols(C_GATE + j * tg, tg)).astype(BF16)
    ba_ref[0] = cols(C_GATE + n_gate, BA_PAD)


def _proj_call(x, mod, sub, gain, w_perm, cos, sin, qg, kg, eq, pq, ek, pk):
    nb, t, d = x.shape
    n_gate = N_BRANCH * d
    ncol = w_perm.shape[1]
    tm = min(512, t)
    nt = t // tm
    row = lambda w: pl.BlockSpec((1, tm, w), lambda b, i: (b, i, 0))
    out_shapes = [
        jax.ShapeDtypeStruct((nb, t, POOL_WIDTH), F32),
        jax.ShapeDtypeStruct((nb, t, DN_CONV_CH), F32),
        jax.ShapeDtypeStruct((nb, t, DN_VW), BF16),
        jax.ShapeDtypeStruct((nb, t, BA_PAD), F32),
        jax.ShapeDtypeStruct((nb, t, Q_W), BF16),
        jax.ShapeDtypeStruct((nb, t, KV_W), BF16),
        jax.ShapeDtypeStruct((nb, t, KV_W), BF16),
        jax.ShapeDtypeStruct((nb, t, n_gate), BF16),
    ]
    out_specs = [row(POOL_WIDTH), row(DN_CONV_CH), row(DN_VW), row(BA_PAD), row(Q_W), row(KV_W),
                 row(KV_W), row(n_gate)]
    kern = functools.partial(_proj_kernel, n_gate=n_gate)
    return pl.pallas_call(
        kern,
        out_shape=out_shapes,
        grid=(nb, nt),
        in_specs=[
            pl.BlockSpec((1, tm, d), lambda b, i: (b, i, 0)),
            pl.BlockSpec((1, 3, 1, d), lambda b, i: (b, sub, 0, 0)),
            _resident((1, d)),
            _resident((d, ncol)),
            pl.BlockSpec((tm, LANE), lambda b, i: (i, 0)),
            pl.BlockSpec((tm, LANE), lambda b, i: (i, 0)),
            _resident((1, Q_W)),
            _resident((1, KV_W)),
            _resident((Q_W, Q_W)),
            _resident((Q_W, Q_W)),
            _resident((KV_W, KV_W)),
            _resident((KV_W, KV_W)),
        ],
        out_specs=out_specs,
        compiler_params=_params("parallel", "parallel"),
        name="mix_proj",
    )(x, mod, gain.reshape(1, d), w_perm, cos, sin, qg, kg, eq, pq, ek, pk)


def _fill_halo(pad_ref, prev_ref, cur_ref, next_ref, tt):
    i = pl.program_id(1)
    n = pl.num_programs(1)
    prev = prev_ref[0]
    nxt = next_ref[0]
    pad_ref[0:SUBLANE, :] = jnp.where(i > 0, prev, jnp.zeros_like(prev))
    pad_ref[SUBLANE:SUBLANE + tt, :] = cur_ref[0]
    pad_ref[SUBLANE + tt:2 * SUBLANE + tt, :] = jnp.where(i < n - 1, nxt, jnp.zeros_like(nxt))


def _halo_specs(tt, width, n_rows):
    per = tt // SUBLANE
    last = n_rows // SUBLANE - 1
    return [
        pl.BlockSpec((1, SUBLANE, width), lambda b, i: (b, jnp.maximum(i * per - 1, 0), 0)),
        pl.BlockSpec((1, tt, width), lambda b, i: (b, i, 0)),
        pl.BlockSpec((1, SUBLANE, width), lambda b, i: (b, jnp.minimum((i + 1) * per, last), 0)),
    ]


def _pool_kernel(prev_ref, cur_ref, next_ref, w_ref, s_ref, o_ref, pad_ref, *, tt, seq):
    _fill_halo(pad_ref, prev_ref, cur_ref, next_ref, tt)
    i = pl.program_id(1)
    t = i * tt + lax.broadcasted_iota(jnp.int32, (tt, 1), 0)
    lane = lax.broadcasted_iota(jnp.int32, (1, POOL_WIDTH), 1)
    cur = pad_ref[SUBLANE:SUBLANE + tt, :]
    y = jnp.zeros((tt, POOL_WIDTH), F32)
    for gi, w in enumerate(POOL_WINDOWS):
        acc = None
        for off in range(-(w // 2), w - w // 2):
            term = pad_ref[SUBLANE + off:SUBLANE + off + tt, :]
            acc = term if acc is None else acc + term
        lo = jnp.maximum(t - w // 2, 0)
        hi = jnp.minimum(t + w - w // 2, seq)
        cnt = (hi - lo).astype(F32)
        in_group = (lane >= gi * POOL_GROUP) & (lane < (gi + 1) * POOL_GROUP)
        y = jnp.where(in_group, acc / cnt - cur, y)
    o_ref[0] = (_dot(y.astype(BF16), w_ref[...]) * s_ref[...]).astype(BF16)


def _pool_call(p, w_bd, scale):
    nb, t, c = p.shape
    tt = min(512, t)
    kern = functools.partial(_pool_kernel, tt=tt, seq=t)
    return pl.pallas_call(
        kern,
        out_shape=jax.ShapeDtypeStruct((nb, t, c), BF16),
        grid=(nb, t // tt),
        in_specs=_halo_specs(tt, c, t) + [_resident((c, c)), _resident((1, c))],
        out_specs=pl.BlockSpec((1, tt, c), lambda b, i: (b, i, 0)),
        scratch_shapes=[pltpu.VMEM((tt + 2 * SUBLANE, c), F32)],
        compiler_params=_params("parallel", "arbitrary"),
        name="pool_mix",
    )(p, p, p, w_bd, scale)


def _conv_kernel(prev_ref, cur_ref, next_ref, w_ref, e_ref, q_ref, k_ref, v_ref, pad_ref, *, tt):
    _fill_halo(pad_ref, prev_ref, cur_ref, next_ref, tt)
    half = DN_CONV // 2
    outs = (q_ref, k_ref, v_ref)
    for part in range(3):
        cs = slice(part * DN_QK, (part + 1) * DN_QK)
        acc = None
        for j in range(DN_CONV):
            lo = SUBLANE + j - half
            term = pad_ref[lo:lo + tt, cs] * w_ref[j:j + 1, cs]
            acc = term if acc is None else acc + term
        y = _silu(acc)
        if part < 2:
            ss = _dot((y * y).astype(BF16), e_ref[...])
            y = y * lax.rsqrt(ss + EPS)
            if part == 0:
                y = y * (DN_DK ** -0.5)
        outs[part][0] = y.astype(BF16)


def _conv_call(qkv, conv_w, e_sum):
    nb, t, c = qkv.shape
    tt = min(512, t)
    kern = functools.partial(_conv_kernel, tt=tt)
    o = jax.ShapeDtypeStruct((nb, t, DN_QK), BF16)
    ospec = pl.BlockSpec((1, tt, DN_QK), lambda b, i: (b, i, 0))
    return pl.pallas_call(
        kern,
        out_shape=[o, o, o],
        grid=(nb, t // tt),
        in_specs=_halo_specs(tt, c, t) + [_resident((DN_CONV, c)), _resident((DN_QK, DN_QK))],
        out_specs=[ospec, ospec, ospec],
        scratch_shapes=[pltpu.VMEM((tt + 2 * SUBLANE, c), F32)],
        compiler_params=_params("parallel", "arbitrary"),
        name="dn_conv",
    )(qkv, qkv, qkv, conv_w, e_sum)


def _delta_kernel(qc_ref, kc_ref, vc_ref, bac_ref, zc_ref, qx_ref, kx_ref, vx_ref, bax_ref, zx_ref,
                  alog_ref, dtb_ref, gn_ref, e_ref, sel_ref, yc_ref, yx_ref,
                  sf_ref, sb_ref, ocf_ref, ocb_ref, oxf_ref, oxb_ref, *, n_ctx, n_lat, group):
    c = DN_CHUNK
    w = DN_HEADS * c
    row = lax.broadcasted_iota(jnp.int32, (w, w), 0)
    col = lax.broadcasted_iota(jnp.int32, (w, w), 1)
    rc_xor = row ^ col
    same = (rc_xor >> 6) == 0
    rp = row & (c - 1)
    cp = col & (c - 1)
    eye = jnp.where(row == col, 1.0, 0.0).astype(F32)
    m_incl = (same & (rp >= cp), same & (rp <= cp))
    m_strict = (same & (rp > cp), same & (rp < cp))
    r64 = lax.broadcasted_iota(jnp.int32, (c, c), 0)
    c64 = lax.broadcasted_iota(jnp.int32, (c, c), 1)
    cum_mat = (jnp.where(r64 >= c64, 1.0, 0.0).astype(BF16), jnp.where(r64 <= c64, 1.0, 0.0).astype(BF16))
    lane = lax.broadcasted_iota(jnp.int32, (c, LANE), 1)
    neg_a = -jnp.exp(alog_ref[...])
    dtb = dtb_ref[...]

    sf_ref[...] = jnp.zeros_like(sf_ref)
    sb_ref[...] = jnp.zeros_like(sb_ref)

    def spread(a):
        return jnp.where(same, jnp.concatenate([a, a, a, a], axis=0), 0.0)

    def gather_heads(o):
        return o[0:c] + o[c:2 * c] + o[2 * c:3 * c] + o[3 * c:4 * c]

    def prepare(refs, chunks):
        q_ref, k_ref, v_ref, ba_ref = refs
        st = []
        for rows, d in chunks:
            q = q_ref[0, rows, :].astype(F32)
            k = k_ref[0, rows, :].astype(F32)
            v = v_ref[0, rows, :].astype(F32)
            ba = ba_ref[0, rows, :]
            g_all = neg_a * jax.nn.softplus(ba + dtb)
            cum = _dot_exact_lhs(cum_mat[d], g_all)
            st.append(dict(d=d, q=q, k=k, v=v, ba=ba, cum=cum))
        for x in st:
            packed = jnp.where(lane < 2 * DN_HEADS, jax.nn.sigmoid(x["ba"]), x["cum"])
            wide = _dot_exact_rhs(packed, sel_ref[x["d"]])
            x["beta"] = wide[:, :w]
            x["g_cum"] = wide[:, w:]
        for x in st:
            x["kb"] = x["k"] * x["beta"]
            k_bd = spread(x["k"]).astype(BF16)
            x["prod"] = _dot_nt(jnp.concatenate([spread(x["kb"]), spread(x["q"])], axis=0).astype(BF16), k_bd)
        for x in st:
            d = x["d"]
            g_col = spread(x["g_cum"])
            decay = jnp.where(m_incl[d], jnp.exp(g_col - g_col.T), 0.0)
            a = jnp.where(m_strict[d], x["prod"][:w] * decay, 0.0)
            x["qkd"] = (x["prod"][w:] * decay).astype(BF16)
            x["a"] = a
            x["t"] = eye - jnp.where((rc_xor >> 1) == 0, a, 0.0)
        for lg in range(1, 6):
            for x in st:
                b_ring = jnp.where((rc_xor >> lg) == 1, x["a"], 0.0).astype(BF16)
                x["t_b"] = x["t"].astype(BF16)
                x["tb"] = _dot(x["t_b"], b_ring).astype(BF16)
            for x in st:
                x["t"] = x["t"] - _dot(x["tb"], x["t_b"])
        out = []
        for x in st:
            last = c - 1 if x["d"] == 0 else 0
            g_cum = x["g_cum"]
            g_tot = g_cum[last:last + 1, :]
            e_g = jnp.exp(g_cum)
            rhs = jnp.concatenate([spread(x["v"] * x["beta"]), spread(x["kb"] * e_g)], axis=1).astype(BF16)
            uw = _dot(x["t"].astype(BF16), rhs)
            out.append(dict(u=uw[:, :w],
                            lhs=jnp.concatenate([uw[:, w:], spread(x["q"] * e_g)], axis=0).astype(BF16),
                            qkd=x["qkd"], kd=spread(x["k"] * jnp.exp(g_tot - g_cum)).astype(BF16),
                            e_tot=jnp.exp(g_tot)))
        return out

    def advance(pre, s_ref):
        s_old = s_ref[...]
        ws_qs = _dot(pre["lhs"], s_old.astype(BF16))
        v_new = (pre["u"] - ws_qs[:w]).astype(BF16)
        o = ws_qs[w:] + _dot(pre["qkd"], v_new)
        s_ref[...] = s_old * pre["e_tot"] + _dot_tn(pre["kd"], v_new)
        return gather_heads(o)

    def run_block(refs, of_ref, ob_ref, rows_f, rows_b):
        chunks = []
        for g in range(len(rows_f)):
            chunks += [(rows_f[g], 0), (rows_b[g], 1)]
        pre = prepare(refs, chunks)
        for g in range(len(rows_f)):
            of_ref[rows_f[g], :] = advance(pre[2 * g], sf_ref)
            ob_ref[rows_b[g], :] = advance(pre[2 * g + 1], sb_ref)

    ctx_refs = (qc_ref, kc_ref, vc_ref, bac_ref)
    lat_refs = (qx_ref, kx_ref, vx_ref, bax_ref)
    for j in range(n_ctx // group):
        run_block(ctx_refs, ocf_ref, ocb_ref,
                  [pl.ds((j * group + g) * c, c) for g in range(group)],
                  [pl.ds((n_ctx - 1 - j * group - g) * c, c) for g in range(group)])

    def lat_body(j, carry):
        rows_f = [pl.ds(pl.multiple_of((j * group + g) * c, c), c) for g in range(group)]
        rows_b = [pl.ds(pl.multiple_of((n_lat - 1 - j * group - g) * c, c), c) for g in range(group)]
        run_block(lat_refs, oxf_ref, oxb_ref, rows_f, rows_b)
        return carry

    lax.fori_loop(0, n_lat // group, lat_body, 0)

    def finish(of_ref, ob_ref, z_ref, y_ref, n_rows):
        tt = min(512, n_rows)
        for j in range(n_rows // tt):
            rs = slice(j * tt, (j + 1) * tt)
            o = of_ref[rs, :] + ob_ref[rs, :]
            ms = _dot((o * o).astype(BF16), e_ref[...])
            y = o * lax.rsqrt(ms + EPS) * gn_ref[...]
            y_ref[0, rs, :] = (y * _silu(z_ref[0, rs, :].astype(F32))).astype(BF16)

    finish(ocf_ref, ocb_ref, zc_ref, yc_ref, n_ctx * c)
    finish(oxf_ref, oxb_ref, zx_ref, yx_ref, n_lat * c)


def _delta_call(ctx_parts, lat_parts, alog, dtb, gn, e_mean, sel):
    qc, kc, vc, bac, zc = ctx_parts
    qx, kx, vx, bax, zx = lat_parts
    nb, lc, wd = qc.shape
    lx = qx.shape[1]
    group = 2
    kern = functools.partial(_delta_kernel, n_ctx=lc // DN_CHUNK, n_lat=lx // DN_CHUNK, group=group)
    seq = lambda n, width: pl.BlockSpec((1, n, width), lambda b: (b, 0, 0))
    in_specs = [seq(lc, wd), seq(lc, wd), seq(lc, wd), seq(lc, BA_PAD), seq(lc, wd),
                seq(lx, wd), seq(lx, wd), seq(lx, wd), seq(lx, BA_PAD), seq(lx, wd),
                _resident((1, LANE)), _resident((1, LANE)), _resident((1, wd)), _resident((wd, wd)),
                _resident(sel.shape)]
    return pl.pallas_call(
        kern,
        out_shape=[jax.ShapeDtypeStruct((nb, lc, wd), BF16), jax.ShapeDtypeStruct((nb, lx, wd), BF16)],
        grid=(nb,),
        in_specs=in_specs,
        out_specs=[seq(lc, wd), seq(lx, wd)],
        scratch_shapes=[pltpu.VMEM((wd, wd), F32), pltpu.VMEM((wd, wd), F32),
                        pltpu.VMEM((lc, wd), F32), pltpu.VMEM((lc, wd), F32),
                        pltpu.VMEM((lx, wd), F32), pltpu.VMEM((lx, wd), F32)],
        compiler_params=_params("parallel"),
        name="delta_rule",
    )(qc, kc, vc, bac, zc, qx, kx, vx, bax, zx, alog, dtb, gn, e_mean, sel)


def _attn_kernel(q_ref, k_ref, vt_ref, o_ref, qs_ref, s_ref, p_ref, al_ref, m_ref, acc_ref, *, tq, tk, n_k):
    for h in range(ATT_GROUP):
        qs_ref[h * tq:(h + 1) * tq, :] = q_ref[0, :, h * ATT_HD:(h + 1) * ATT_HD]
    m_ref[...] = jnp.full(m_ref.shape, -1e30, F32)
    acc_ref[...] = jnp.zeros(acc_ref.shape, F32)

    def scores(j, slot):
        s_ref[slot] = _dot_nt(k_ref[0, 0, j * tk:(j + 1) * tk, :], qs_ref[...])

    def softmax(slot):
        s = s_ref[slot]
        m_old = m_ref[...]
        m_new = jnp.maximum(m_old, jnp.max(s, axis=0, keepdims=True))
        p_ref[slot] = jnp.exp2(s - m_new).astype(BF16)
        al_ref[slot] = jnp.exp2(m_old - m_new)
        m_ref[...] = m_new

    def values(j, slot):
        acc_ref[...] = acc_ref[...] * al_ref[slot] + _dot(vt_ref[0, 0, :, j * tk:(j + 1) * tk], p_ref[slot])

    for i in range(n_k + 2):
        if i < n_k:
            scores(i, i % 2)
        if 1 <= i <= n_k:
            softmax((i - 1) % 2)
        if i >= 2:
            values(i - 2, i % 2)

    acc = acc_ref[...]
    out = acc[:ATT_HD] / acc[ATT_HD:ATT_HD + 1]
    o_ref[0] = jnp.concatenate([out[:, h * tq:(h + 1) * tq].T for h in range(ATT_GROUP)], axis=1).astype(BF16)


def _attn_call(q, k4, vt):
    nb, lq, _ = q.shape
    s = k4.shape[2]
    tq = min(256, lq)
    tk = 256
    gw = ATT_GROUP * ATT_HD
    cols = ATT_GROUP * tq
    kern = functools.partial(_attn_kernel, tq=tq, tk=tk, n_k=s // tk)
    return pl.pallas_call(
        kern,
        out_shape=jax.ShapeDtypeStruct((nb, lq, Q_W), BF16),
        grid=(nb, ATT_KV, lq // tq),
        in_specs=[
            pl.BlockSpec((1, tq, gw), lambda b, g, i: (b, i, g)),
            pl.BlockSpec((1, 1, s, ATT_HD), lambda b, g, i: (b, g, 0, 0)),
            pl.BlockSpec((1, 1, V_ROWS, s), lambda b, g, i: (b, g, 0, 0)),
        ],
        out_specs=pl.BlockSpec((1, tq, gw), lambda b, g, i: (b, i, g)),
        scratch_shapes=[pltpu.VMEM((cols, ATT_HD), BF16), pltpu.VMEM((2, tk, cols), F32),
                        pltpu.VMEM((2, tk, cols), BF16), pltpu.VMEM((2, 1, cols), F32),
                        pltpu.VMEM((1, cols), F32), pltpu.VMEM((V_ROWS, cols), F32)],
        compiler_params=_params("parallel", "parallel", "arbitrary"),
        name="gqa_attn",
    )(q, k4, vt)


def _merge_kernel(x_ref, mod_ref, yp_ref, yd_ref, ya_ref, gate_ref, wb_ref, wo_ref, o_ref):
    d = x_ref.shape[2]
    a0 = POOL_WIDTH
    a1 = POOL_WIDTH + DN_VW
    m = gate_ref[0, :, 0:d].astype(F32) * _dot(yp_ref[0], wb_ref[0:a0, :])
    m = m + gate_ref[0, :, d:2 * d].astype(F32) * _dot(yd_ref[0], wb_ref[a0:a1, :])
    m = m + gate_ref[0, :, 2 * d:3 * d].astype(F32) * _dot(ya_ref[0], wb_ref[a1:, :])
    o_ref[0] = x_ref[0] + mod_ref[0, 2] * _dot(m.astype(BF16), wo_ref[...])


def _merge_call(x, mod, sub, y_pool, y_dn, y_att, gate, w_branch, w_out):
    nb, t, d = x.shape
    tm = min(512, t)
    row = lambda w: pl.BlockSpec((1, tm, w), lambda b, i: (b, i, 0))
    return pl.pallas_call(
        _merge_kernel,
        out_shape=jax.ShapeDtypeStruct(x.shape, F32),
        grid=(nb, t // tm),
        in_specs=[row(d), pl.BlockSpec((1, 3, 1, d), lambda b, i: (b, sub, 0, 0)),
                  row(POOL_WIDTH), row(DN_VW), row(Q_W), row(N_BRANCH * d),
                  _resident(w_branch.shape), _resident(w_out.shape)],
        out_specs=row(d),
        compiler_params=_params("parallel", "parallel"),
        name="merge",
    )(x, mod, y_pool, y_dn, y_att, gate, w_branch, w_out)


def _final_norm_kernel(x_ref, g_ref, o_ref):
    x = x_ref[0]
    ms = jnp.mean(x * x, axis=-1, keepdims=True)
    o_ref[0] = x * lax.rsqrt(ms + EPS) * g_ref[...]


def _final_norm_call(x, gain):
    nb, t, d = x.shape
    tm = min(1024, t)
    return pl.pallas_call(
        _final_norm_kernel,
        out_shape=jax.ShapeDtypeStruct(x.shape, F32),
        grid=(nb, t // tm),
        in_specs=[pl.BlockSpec((1, tm, d), lambda b, i: (b, i, 0)), _resident((1, d))],
        out_specs=pl.BlockSpec((1, tm, d), lambda b, i: (b, i, 0)),
        compiler_params=_params("parallel", "parallel"),
        name="final_norm",
    )(x, gain.reshape(1, d))


def _block_diag_const(width, block, value):
    idx = np.arange(width)
    return jnp.asarray(np.where((idx[:, None] // block) == (idx[None, :] // block), value, 0.0), BF16)


def _pair_swap_const(width):
    idx = np.arange(width)
    return jnp.asarray((idx[:, None] == (idx[None, :] ^ 1)).astype(np.float32), BF16)


def _rope_tables(n_tokens):
    rows = n_tokens // GRID_W
    row = jnp.broadcast_to(jnp.arange(rows)[:, None], (rows, GRID_W)).reshape(-1)
    col = jnp.broadcast_to(jnp.arange(GRID_W)[None, :], (rows, GRID_W)).reshape(-1)
    n_freq = ATT_HD // 4
    inv = ROPE_BASE ** (-jnp.arange(n_freq, dtype=F32) / n_freq)
    ang = jnp.concatenate([row[:, None].astype(F32) * inv, col[:, None].astype(F32) * inv], axis=-1)
    cos = jnp.repeat(jnp.cos(ang), 2, axis=1)
    sin = jnp.repeat(jnp.sin(ang), 2, axis=1)
    sign = jnp.where(jnp.arange(ATT_HD) % 2 == 0, -1.0, 1.0).astype(F32)
    sin = sin * sign
    reps = LANE // ATT_HD
    return jnp.tile(cos, (1, reps)), jnp.tile(sin, (1, reps))


def _permute_w_in(w_in):
    d = w_in.shape[0]
    o_pool = 0
    o_qkv = o_pool + POOL_WIDTH
    o_z = o_qkv + DN_CONV_CH
    o_ba = o_z + DN_VW
    o_aq = o_ba + 4 * DN_HEADS
    o_gate = o_aq + Q_W + 2 * KV_W
    pad = jnp.zeros((d, BA_PAD - 4 * DN_HEADS), w_in.dtype)
    return jnp.concatenate([w_in[:, o_pool:o_ba], w_in[:, o_aq:], w_in[:, o_ba:o_aq], pad], axis=1).astype(BF16)


def _head_select_const():
    sel = np.zeros((2, LANE, 2 * DN_VW), np.float32)
    for d in range(2):
        for h in range(DN_HEADS):
            sel[d, d * DN_HEADS + h, h * DN_DK:(h + 1) * DN_DK] = 1.0
            sel[d, 2 * DN_HEADS + d * DN_HEADS + h, DN_VW + h * DN_DK:DN_VW + (h + 1) * DN_DK] = 1.0
    return jnp.asarray(sel, BF16)


def _lane_vec(vals, offset):
    out = jnp.zeros((1, LANE), F32)
    return out.at[0, offset:offset + vals.size].set(vals.reshape(-1).astype(F32))


def _kv_layout(k_c, v_c, k_x=None, v_x=None):
    k = k_c if k_x is None else jnp.concatenate([k_c, k_x], axis=1)
    v = v_c if v_x is None else jnp.concatenate([v_c, v_x], axis=1)
    nb, s, _ = k.shape
    k4 = k.reshape(nb, s, ATT_KV, ATT_HD).transpose(0, 2, 1, 3)
    vt = v.reshape(nb, s, ATT_KV, ATT_HD).transpose(0, 2, 3, 1)
    ones = jnp.ones((nb, ATT_KV, 1, s), vt.dtype)
    zeros = jnp.zeros((nb, ATT_KV, V_ROWS - ATT_HD - 1, s), vt.dtype)
    return k4, jnp.concatenate([vt, ones, zeros], axis=2)


def kernel(x, c, ctx, c_ctx, w_ada, b_ada, norm_ffn1, w_ffn1_in, w_ffn1_out, norm_mix, w_in, pool_w, pool_scale, dn_conv, dn_a_log, dn_dt_bias, dn_norm, q_norm, k_norm, w_branch, w_out, norm_ffn2, w_ffn2_in, w_ffn2_out, norm_final):
    nb, seq, d = x.shape
    lc = ctx.shape[1]
    depth = w_ada.shape[0]

    n_rows = -(-(nb + 1) // SUBLANE) * SUBLANE
    cond = jnp.concatenate([c, c_ctx[None], jnp.zeros((n_rows - nb - 1, d), F32)], axis=0)
    mod_all = _ada_call(cond, w_ada, b_ada).reshape(depth, n_rows, N_MOD, 1, d)

    cos_x, sin_x = _rope_tables(seq)
    cos_c = jnp.ones((nb * lc, LANE), F32)
    sin_c = jnp.zeros((nb * lc, LANE), F32)
    e_q = _block_diag_const(Q_W, ATT_HD, 1.0 / ATT_HD)
    e_k = _block_diag_const(KV_W, ATT_HD, 1.0 / ATT_HD)
    p_q = _pair_swap_const(Q_W)
    p_k = _pair_swap_const(KV_W)
    e_sum = _block_diag_const(DN_QK, DN_DK, 1.0)
    e_mean = _block_diag_const(DN_VW, DN_DV, 1.0 / DN_DV)
    head_sel = _head_select_const()

    xc = ctx.reshape(1, nb * lc, d)
    for l in range(depth):
        ctx_out = l < depth - 1
        mod_x = mod_all[l, :nb]
        mod_c = mod_all[l, nb:nb + 1]
        wf1_in = w_ffn1_in[l].astype(BF16)
        wf1_out = w_ffn1_out[l].astype(BF16)
        wf2_in = w_ffn2_in[l].astype(BF16)
        wf2_out = w_ffn2_out[l].astype(BF16)
        w_perm = _permute_w_in(w_in[l])
        w_br = w_branch[l].astype(BF16)
        w_o = w_out[l].astype(BF16)
        qg = jnp.tile(q_norm[l], ATT_HEADS).reshape(1, Q_W)
        kg = jnp.tile(k_norm[l], ATT_KV).reshape(1, KV_W)
        pool_bd = jax.scipy.linalg.block_diag(*[pool_w[l, g] for g in range(len(POOL_WINDOWS))]).astype(BF16)
        pool_s = pool_scale[l].reshape(1, POOL_WIDTH)
        alog = _lane_vec(dn_a_log[l], 2 * DN_HEADS)
        dtb = _lane_vec(dn_dt_bias[l], 2 * DN_HEADS)
        gn = jnp.tile(dn_norm[l], DN_HEADS).reshape(1, DN_VW)

        x = _ffn_call(x, mod_x, 0, norm_ffn1[l], wf1_in, wf1_out)
        xc = _ffn_call(xc, mod_c, 0, norm_ffn1[l], wf1_in, wf1_out)

        px = _proj_call(x, mod_x, 1, norm_mix[l], w_perm, cos_x, sin_x, qg, kg, e_q, p_q, e_k, p_k)
        pc = _proj_call(xc, mod_c, 1, norm_mix[l], w_perm, cos_c, sin_c, qg, kg, e_q, p_q, e_k, p_k)
        pool_x, qkv_x, z_x, ba_x, aq_x, ak_x, av_x, gate_x = px
        pool_c, qkv_c, z_c, ba_c, aq_c, ak_c, av_c, gate_c = [t.reshape(nb, lc, t.shape[-1]) for t in pc]

        yp_x = _pool_call(pool_x, pool_bd, pool_s)
        qd_x, kd_x, vd_x = _conv_call(qkv_x, dn_conv[l], e_sum)
        qd_c, kd_c, vd_c = _conv_call(qkv_c, dn_conv[l], e_sum)
        yd_c, yd_x = _delta_call((qd_c, kd_c, vd_c, ba_c, z_c), (qd_x, kd_x, vd_x, ba_x, z_x),
                                 alog, dtb, gn, e_mean, head_sel)

        k4_all, vt_all = _kv_layout(ak_c, av_c, ak_x, av_x)
        ya_x = _attn_call(aq_x, k4_all, vt_all)
        x = _merge_call(x, mod_x, 1, yp_x, yd_x, ya_x, gate_x, w_br, w_o)

        if ctx_out:
            yp_c = _pool_call(pool_c, pool_bd, pool_s)
            k4_c, vt_c = _kv_layout(ak_c, av_c)
            ya_c = _attn_call(aq_c, k4_c, vt_c)
            flat = lambda t: t.reshape(1, nb * lc, t.shape[-1])
            xc = _merge_call(xc, mod_c, 1, flat(yp_c), flat(yd_c), flat(ya_c), flat(gate_c), w_br, w_o)

        x = _ffn_call(x, mod_x, 2, norm_ffn2[l], wf2_in, wf2_out)
        if ctx_out:
            xc = _ffn_call(xc, mod_c, 2, norm_ffn2[l], wf2_in, wf2_out)

    return _final_norm_call(x, norm_final)
```

```python
import functools
import math

import jax
import jax.numpy as jnp
import numpy as np
from jax import lax
from jax.experimental import pallas as pl
from jax.experimental.pallas import tpu as pltpu

F32 = jnp.float32
BF16 = jnp.bfloat16

N_MOD = 9
FFN_RES = 0.5
POOL_WIDTH = 256
POOL_WINDOWS = (2, 4, 8, 16)
POOL_GROUP = 64
DN_HEADS = 4
DN_DK = 64
DN_DV = 64
DN_CONV = 5
DN_CHUNK = 64
DN_QK = DN_HEADS * DN_DK
DN_VW = DN_HEADS * DN_DV
DN_CONV_CH = 2 * DN_QK + DN_VW
ATT_HEADS = 8
ATT_KV = 2
ATT_GROUP = ATT_HEADS // ATT_KV
ATT_HD = 64
ROPE_BASE = 10000.0
GRID_W = 64
N_BRANCH = 3
EPS = 1e-6
LOG2E = 1.4426950408889634

LANE = 128
SUBLANE = 8
VMEM_LIMIT = 56 * 1024 * 1024

BA_PAD = LANE
Q_W = ATT_HEADS * ATT_HD
KV_W = ATT_KV * ATT_HD
V_ROWS = ATT_HD + 16
C_POOL = 0
C_QKV = C_POOL + POOL_WIDTH
C_Z = C_QKV + DN_CONV_CH
C_AQ = C_Z + DN_VW
C_AK = C_AQ + Q_W
C_AV = C_AK + KV_W
C_GATE = C_AV + KV_W


def _params(*sem):
    return pltpu.CompilerParams(dimension_semantics=sem, vmem_limit_bytes=VMEM_LIMIT)


def _tile(n, target):
    t = min(target, n)
    assert n % t == 0 and t % SUBLANE == 0, (n, t)
    return t


def _resident(shape):
    nd = len(shape)
    return pl.BlockSpec(shape, lambda *_: (0,) * nd, pipeline_mode=pl.Buffered(1))


def _silu(x):
    return x * jax.nn.sigmoid(x)


def _dot(a, b):
    return jnp.dot(a, b, preferred_element_type=F32)


def _dot_nt(a, b):
    return lax.dot_general(a, b, (((1,), (1,)), ((), ())), preferred_element_type=F32)


def _dot_tn(a, b):
    return lax.dot_general(a, b, (((0,), (0,)), ((), ())), preferred_element_type=F32)


def _split3(x):
    hi = x.astype(BF16)
    r1 = x - hi.astype(F32)
    mid = r1.astype(BF16)
    lo = (r1 - mid.astype(F32)).astype(BF16)
    return hi, mid, lo


def _dot_exact_lhs(a_bf16, x_f32):
    hi, mid, lo = _split3(x_f32)
    return _dot(a_bf16, hi) + _dot(a_bf16, mid) + _dot(a_bf16, lo)


def _dot_exact_rhs(x_f32, b_bf16):
    hi, mid, lo = _split3(x_f32)
    return _dot(hi, b_bf16) + _dot(mid, b_bf16) + _dot(lo, b_bf16)


def _ada_kernel(s_ref, w_ref, b_ref, o_ref):
    s = _silu(s_ref[...])
    w = w_ref[0]
    s_hi, s_mid, s_lo = _split3(s)
    w_hi, w_mid, w_lo = _split3(w)
    acc = _dot(s_hi, w_hi)
    acc = acc + _dot(s_hi, w_mid) + _dot(s_mid, w_hi)
    acc = acc + _dot(s_hi, w_lo) + _dot(s_mid, w_mid) + _dot(s_lo, w_hi)
    o_ref[0] = acc + b_ref[0]


def _ada_call(cond, w_ada, b_ada):
    depth, d, n = w_ada.shape
    rows = cond.shape[0]
    tn = 1024
    return pl.pallas_call(
        _ada_kernel,
        out_shape=jax.ShapeDtypeStruct((depth, rows, n), F32),
        grid=(depth, n // tn),
        in_specs=[
            pl.BlockSpec((rows, d), lambda l, j: (0, 0)),
            pl.BlockSpec((1, d, tn), lambda l, j: (l, 0, j)),
            pl.BlockSpec((1, 1, tn), lambda l, j: (l, 0, j)),
        ],
        out_specs=pl.BlockSpec((1, rows, tn), lambda l, j: (l, 0, j)),
        compiler_params=_params("parallel", "parallel"),
        name="ada_mod",
    )(cond, w_ada, b_ada.reshape(depth, 1, n))


def _norm_mod(x, gain, shift, scale):
    ms = jnp.mean(x * x, axis=-1, keepdims=True)
    h = x * lax.rsqrt(ms + EPS) * gain
    return h * (1.0 + scale) + shift


def _ffn_kernel(x_ref, mod_ref, g_ref, wi_ref, wo_ref, o_ref, acc_ref, *, d_ff, tf):
    x = x_ref[0]
    h = _norm_mod(x, g_ref[...], mod_ref[0, 0], mod_ref[0, 1]).astype(BF16)
    for j in range(d_ff // tf):
        g = _dot(h, wi_ref[:, j * tf:(j + 1) * tf])
        u = _dot(h, wi_ref[:, d_ff + j * tf:d_ff + (j + 1) * tf])
        a = (_silu(g) * u).astype(BF16)
        part = _dot(a, wo_ref[j * tf:(j + 1) * tf, :])
        if j == 0:
            acc_ref[...] = part
        else:
            acc_ref[...] += part
    o_ref[0] = x + (FFN_RES * mod_ref[0, 2]) * acc_ref[...]


def _ffn_call(x, mod, sub, gain, w_in, w_out):
    nb, t, d = x.shape
    d_ff = w_out.shape[0]
    tm = _tile(t, 512)
    tf = 256
    kern = functools.partial(_ffn_kernel, d_ff=d_ff, tf=tf)
    return pl.pallas_call(
        kern,
        out_shape=jax.ShapeDtypeStruct(x.shape, F32),
        grid=(nb, t // tm),
        in_specs=[
            pl.BlockSpec((1, tm, d), lambda b, i: (b, i, 0)),
            pl.BlockSpec((1, 3, 1, d), lambda b, i: (b, sub, 0, 0)),
            _resident((1, d)),
            _resident((d, 2 * d_ff)),
            _resident((d_ff, d)),
        ],
        out_specs=pl.BlockSpec((1, tm, d), lambda b, i: (b, i, 0)),
        scratch_shapes=[pltpu.VMEM((tm, d), F32)],
        compiler_params=_params("parallel", "parallel"),
        name="ffn",
    )(x, mod, gain.reshape(1, d), w_in, w_out)


def _head_norm_rope(t_raw, gain, e_mat, p_mat, cos, sin, out_scale):
    ms = _dot((t_raw * t_raw).astype(BF16), e_mat)
    r = lax.rsqrt(ms + EPS) * out_scale
    t = t_raw * gain
    t_sw = _dot(t.astype(BF16), p_mat)
    w = t_raw.shape[1]
    outs = []
    for g in range(w // LANE):
        sl = slice(g * LANE, (g + 1) * LANE)
        outs.append(r[:, sl] * (t[:, sl] * cos + t_sw[:, sl] * sin))
    return jnp.concatenate(outs, axis=1) if len(outs) > 1 else outs[0]


def _proj_kernel(xp_ref, x_ref, xn_ref, mod_ref, g_ref, w_ref, cos_ref, sin_ref, qg_ref, kg_ref,
                 eq_ref, pq_ref, ek_ref, pk_ref, cw_ref, es_ref,
                 pool_ref, dq_ref, dk_ref, dv_ref, z_ref, ba_ref, q_ref, k_ref, v_ref, gate_ref,
                 pad_ref, *, n_gate, tm):
    gain = g_ref[...]
    shift = mod_ref[0, 0]
    scale = mod_ref[0, 1]
    h = _norm_mod(x_ref[0], gain, shift, scale).astype(BF16)

    def cols(lo, width):
        return _dot(h, w_ref[:, lo:lo + width])

    pool_ref[0] = cols(C_POOL, POOL_WIDTH)

    i = pl.program_id(1)
    n = pl.num_programs(1)
    halo = jnp.concatenate([xp_ref[0], xn_ref[0]], axis=0)
    h_halo = _norm_mod(halo, gain, shift, scale).astype(BF16)
    w_qkv = w_ref[:, C_QKV:C_QKV + DN_CONV_CH]
    qkv_halo = _dot(h_halo, w_qkv)
    pad_ref[0:SUBLANE, :] = jnp.where(i > 0, qkv_halo[0:SUBLANE], 0.0)
    pad_ref[SUBLANE + tm:2 * SUBLANE + tm, :] = jnp.where(i < n - 1, qkv_halo[SUBLANE:2 * SUBLANE], 0.0)
    for j in range(DN_CONV_CH // 256):
        pad_ref[SUBLANE:SUBLANE + tm, j * 256:(j + 1) * 256] = cols(C_QKV + j * 256, 256)
    half = DN_CONV // 2
    for part, o_ref in enumerate((dq_ref, dk_ref, dv_ref)):
        cs = slice(part * DN_QK, (part + 1) * DN_QK)
        acc = None
        for j in range(DN_CONV):
            lo = SUBLANE + j - half
            term = pad_ref[lo:lo + tm, cs] * cw_ref[j:j + 1, cs]
            acc = term if acc is None else acc + term
        y = _silu(acc)
        if part < 2:
            y = y * lax.rsqrt(_dot((y * y).astype(BF16), es_ref[...]) + EPS)
            if part == 0:
                y = y * (DN_DK ** -0.5)
        o_ref[0] = y.astype(BF16)

    z_ref[0] = cols(C_Z, DN_VW).astype(BF16)
    cos = cos_ref[...]
    sin = sin_ref[...]
    q = _head_norm_rope(cols(C_AQ, Q_W), qg_ref[...], eq_ref[...], pq_ref[...], cos, sin,
                        (ATT_HD ** -0.5) * LOG2E)
    q_ref[0] = q.astype(BF16)
    k = _head_norm_rope(cols(C_AK, KV_W), kg_ref[...], ek_ref[...], pk_ref[...], cos, sin, 1.0)
    k_ref[0] = k.astype(BF16)
    v_ref[0] = cols(C_AV, KV_W).astype(BF16)
    tg = 512
    for j in range(n_gate // tg):
        gate_ref[0, :, j * tg:(j + 1) * tg] = jax.nn.sigmoid(cols(C_GATE + j * tg, tg)).astype(BF16)
    ba_ref[0] = cols(C_GATE + n_gate, BA_PAD)


def _mod_spec(mod, nb, sub, d):
    if mod.shape[0] == nb:
        return pl.BlockSpec((1, 3, 1, d), lambda b, i: (b, sub, 0, 0))
    return pl.BlockSpec((1, 3, 1, d), lambda b, i: (0, sub, 0, 0))


def _proj_call(x, mod, sub, gain, w_perm, cos, sin, qg, kg, eq, pq, ek, pk, conv_w, e_sum):
    nb, t, d = x.shape
    n_gate = N_BRANCH * d
    ncol = w_perm.shape[1]
    tm = _tile(t, 512)
    nt = t // tm
    row = lambda w: pl.BlockSpec((1, tm, w), lambda b, i: (b, i, 0))
    widths = [(POOL_WIDTH, F32), (DN_QK, BF16), (DN_QK, BF16), (DN_VW, BF16), (DN_VW, BF16), (BA_PAD, F32),
              (Q_W, BF16), (KV_W, BF16), (KV_W, BF16), (n_gate, BF16)]
    kern = functools.partial(_proj_kernel, n_gate=n_gate, tm=tm)
    return pl.pallas_call(
        kern,
        out_shape=[jax.ShapeDtypeStruct((nb, t, wd), dt) for wd, dt in widths],
        grid=(nb, nt),
        in_specs=_halo_specs(tm, d, t) + [
            _mod_spec(mod, nb, sub, d),
            _resident((1, d)),
            _resident((d, ncol)),
            pl.BlockSpec((tm, LANE), lambda b, i: (i, 0)),
            pl.BlockSpec((tm, LANE), lambda b, i: (i, 0)),
            _resident((1, Q_W)),
            _resident((1, KV_W)),
            _resident((Q_W, Q_W)),
            _resident((Q_W, Q_W)),
            _resident((KV_W, KV_W)),
            _resident((KV_W, KV_W)),
            _resident(conv_w.shape),
            _resident(e_sum.shape),
        ],
        out_specs=[row(wd) for wd, _ in widths],
        scratch_shapes=[pltpu.VMEM((tm + 2 * SUBLANE, DN_CONV_CH), F32)],
        compiler_params=_params("parallel", "arbitrary"),
        name="mix_proj",
    )(x, x, x, mod, gain.reshape(1, d), w_perm, cos, sin, qg, kg, eq, pq, ek, pk, conv_w, e_sum)


def _fill_halo(pad_ref, prev_ref, cur_ref, next_ref, tt):
    i = pl.program_id(1)
    n = pl.num_programs(1)
    prev = prev_ref[0]
    nxt = next_ref[0]
    pad_ref[0:SUBLANE, :] = jnp.where(i > 0, prev, jnp.zeros_like(prev))
    pad_ref[SUBLANE:SUBLANE + tt, :] = cur_ref[0]
    pad_ref[SUBLANE + tt:2 * SUBLANE + tt, :] = jnp.where(i < n - 1, nxt, jnp.zeros_like(nxt))


def _halo_specs(tt, width, n_rows):
    per = tt // SUBLANE
    last = n_rows // SUBLANE - 1
    return [
        pl.BlockSpec((1, SUBLANE, width), lambda b, i: (b, jnp.maximum(i * per - 1, 0), 0)),
        pl.BlockSpec((1, tt, width), lambda b, i: (b, i, 0)),
        pl.BlockSpec((1, SUBLANE, width), lambda b, i: (b, jnp.minimum((i + 1) * per, last), 0)),
    ]


def _window_means_minus_self(pad_ref, tt, seq):
    i = pl.program_id(1)
    t = i * tt + lax.broadcasted_iota(jnp.int32, (tt, 1), 0)
    lane = lax.broadcasted_iota(jnp.int32, (1, POOL_WIDTH), 1)
    cur = pad_ref[SUBLANE:SUBLANE + tt, :]
    y = jnp.zeros((tt, POOL_WIDTH), F32)
    for gi, w in enumerate(POOL_WINDOWS):
        acc = None
        for off in range(-(w // 2), w - w // 2):
            term = pad_ref[SUBLANE + off:SUBLANE + off + tt, :]
            acc = term if acc is None else acc + term
        lo = jnp.maximum(t - w // 2, 0)
        hi = jnp.minimum(t + w - w // 2, seq)
        cnt = (hi - lo).astype(F32)
        in_group = (lane >= gi * POOL_GROUP) & (lane < (gi + 1) * POOL_GROUP)
        y = jnp.where(in_group, acc / cnt - cur, y)
    return y


def _delta_kernel(qc_ref, kc_ref, vc_ref, bac_ref, zc_ref, qx_ref, kx_ref, vx_ref, bax_ref, zx_ref,
                  alog_ref, dtb_ref, gn_ref, e_ref, sel_ref, yc_ref, yx_ref,
                  sf_ref, sb_ref, ocf_ref, ocb_ref, oxf_ref, oxb_ref, *, n_ctx, n_lat, group):
    c = DN_CHUNK
    w = DN_HEADS * c
    row = lax.broadcasted_iota(jnp.int32, (w, w), 0)
    col = lax.broadcasted_iota(jnp.int32, (w, w), 1)
    rc_xor = row ^ col
    same = (rc_xor >> 6) == 0
    rp = row & (c - 1)
    cp = col & (c - 1)
    eye = jnp.where(row == col, 1.0, 0.0).astype(F32)
    m_incl = (same & (rp >= cp), same & (rp <= cp))
    m_strict = (same & (rp > cp), same & (rp < cp))
    r64 = lax.broadcasted_iota(jnp.int32, (c, c), 0)
    c64 = lax.broadcasted_iota(jnp.int32, (c, c), 1)
    cum_mat = (jnp.where(r64 >= c64, 1.0, 0.0).astype(BF16), jnp.where(r64 <= c64, 1.0, 0.0).astype(BF16))
    neg_a = -jnp.exp(alog_ref[...])
    dtb = dtb_ref[...]

    sf_ref[...] = jnp.zeros_like(sf_ref)
    sb_ref[...] = jnp.zeros_like(sb_ref)

    def spread(a):
        return jnp.where(same, jnp.concatenate([a, a, a, a], axis=0), 0.0)

    def gather_heads(o):
        return o[0:c] + o[c:2 * c] + o[2 * c:3 * c] + o[3 * c:4 * c]

    def prepare(refs, chunks):
        q_ref, k_ref, v_ref, ba_ref = refs
        st = []
        for rows, d in chunks:
            q = q_ref[0, rows, :].astype(F32)
            k = k_ref[0, rows, :].astype(F32)
            v = v_ref[0, rows, :].astype(F32)
            ba = ba_ref[0, rows, :]
            g_all = neg_a * jax.nn.softplus(ba + dtb)
            cum = _dot_exact_lhs(cum_mat[d], g_all)
            st.append(dict(d=d, q=q, k=k, v=v, ba=ba, cum=cum))
        for x in st:
            sel = sel_ref[x["d"]]
            x["beta"] = _dot(jax.nn.sigmoid(x["ba"]).astype(BF16), sel[:, :w])
            x["g_cum"] = _dot_exact_rhs(x["cum"], sel[:, w:])
        for x in st:
            x["kb"] = x["k"] * x["beta"]
            k_bd = spread(x["k"]).astype(BF16)
            x["prod"] = _dot_nt(jnp.concatenate([spread(x["kb"]), spread(x["q"])], axis=0).astype(BF16), k_bd)
        for x in st:
            d = x["d"]
            g_col = spread(x["g_cum"])
            decay = jnp.where(m_incl[d], jnp.exp(g_col - g_col.T), 0.0)
            a = jnp.where(m_strict[d], x["prod"][:w] * decay, 0.0)
            x["qkd"] = (x["prod"][w:] * decay).astype(BF16)
            x["a"] = a
            a4 = jnp.where((rc_xor >> 2) == 0, a, 0.0)
            x["t0"] = eye - a4
            x["a4"] = a4.astype(BF16)
        for x in st:
            x["a4sq"] = _dot(x["a4"], x["a4"]).astype(BF16)
        for x in st:
            x["t"] = x["t0"] + _dot(x["t0"].astype(BF16), x["a4sq"])
        for lg in range(2, 6):
            for x in st:
                b_ring = jnp.where((rc_xor >> lg) == 1, x["a"], 0.0).astype(BF16)
                x["t_b"] = x["t"].astype(BF16)
                x["tb"] = _dot(x["t_b"], b_ring).astype(BF16)
            for x in st:
                x["t"] = x["t"] - _dot(x["tb"], x["t_b"])
        out = []
        for x in st:
            last = c - 1 if x["d"] == 0 else 0
            g_cum = x["g_cum"]
            g_tot = g_cum[last:last + 1, :]
            e_g = jnp.exp(g_cum)
            rhs = jnp.concatenate([spread(x["v"] * x["beta"]), spread(x["kb"] * e_g)], axis=1).astype(BF16)
            uw = _dot(x["t"].astype(BF16), rhs)
            out.append(dict(u=uw[:, :w],
                            lhs=jnp.concatenate([uw[:, w:], spread(x["q"] * e_g)], axis=0).astype(BF16),
                            qkd=x["qkd"], kd=spread(x["k"] * jnp.exp(g_tot - g_cum)).astype(BF16),
                            e_tot=jnp.exp(g_tot)))
        return out

    def advance(pre, s_ref):
        s_old = s_ref[...]
        ws_qs = _dot(pre["lhs"], s_old.astype(BF16))
        v_new = (pre["u"] - ws_qs[:w]).astype(BF16)
        o = ws_qs[w:] + _dot(pre["qkd"], v_new)
        s_ref[...] = s_old * pre["e_tot"] + _dot_tn(pre["kd"], v_new)
        return gather_heads(o)

    def run_block(refs, of_ref, ob_ref, rows_f, rows_b):
        chunks = []
        for g in range(len(rows_f)):
            chunks += [(rows_f[g], 0), (rows_b[g], 1)]
        pre = prepare(refs, chunks)
        for g in range(len(rows_f)):
            of_ref[rows_f[g], :] = advance(pre[2 * g], sf_ref)
            ob_ref[rows_b[g], :] = advance(pre[2 * g + 1], sb_ref)

    ctx_refs = (qc_ref, kc_ref, vc_ref, bac_ref)
    lat_refs = (qx_ref, kx_ref, vx_ref, bax_ref)
    for j in range(n_ctx // group):
        run_block(ctx_refs, ocf_ref, ocb_ref,
                  [pl.ds((j * group + g) * c, c) for g in range(group)],
                  [pl.ds((n_ctx - 1 - j * group - g) * c, c) for g in range(group)])

    def lat_body(j, carry):
        rows_f = [pl.ds(pl.multiple_of((j * group + g) * c, c), c) for g in range(group)]
        rows_b = [pl.ds(pl.multiple_of((n_lat - 1 - j * group - g) * c, c), c) for g in range(group)]
        run_block(lat_refs, oxf_ref, oxb_ref, rows_f, rows_b)
        return carry

    lax.fori_loop(0, n_lat // group, lat_body, 0)

    def finish(of_ref, ob_ref, z_ref, y_ref, n_rows):
        tt = min(512, n_rows)
        for j in range(n_rows // tt):
            rs = slice(j * tt, (j + 1) * tt)
            o = of_ref[rs, :] + ob_ref[rs, :]
            ms = _dot((o * o).astype(BF16), e_ref[...])
            y = o * lax.rsqrt(ms + EPS) * gn_ref[...]
            y_ref[0, rs, :] = (y * _silu(z_ref[0, rs, :].astype(F32))).astype(BF16)

    finish(ocf_ref, ocb_ref, zc_ref, yc_ref, n_ctx * c)
    finish(oxf_ref, oxb_ref, zx_ref, yx_ref, n_lat * c)


def _delta_call(ctx_parts, lat_parts, alog, dtb, gn, e_mean, sel):
    qc, kc, vc, bac, zc = ctx_parts
    qx, kx, vx, bax, zx = lat_parts
    nb, lc, wd = qc.shape
    lx = qx.shape[1]
    group = 2
    assert lc % (group * DN_CHUNK) == 0 and lx % (group * DN_CHUNK) == 0, (lc, lx)
    assert lc % _tile(lc, 512) == 0 and lx % _tile(lx, 512) == 0
    kern = functools.partial(_delta_kernel, n_ctx=lc // DN_CHUNK, n_lat=lx // DN_CHUNK, group=group)
    seq = lambda n, width: pl.BlockSpec((1, n, width), lambda b: (b, 0, 0))
    in_specs = [seq(lc, wd), seq(lc, wd), seq(lc, wd), seq(lc, BA_PAD), seq(lc, wd),
                seq(lx, wd), seq(lx, wd), seq(lx, wd), seq(lx, BA_PAD), seq(lx, wd),
                _resident((1, LANE)), _resident((1, LANE)), _resident((1, wd)), _resident((wd, wd)),
                _resident(sel.shape)]
    return pl.pallas_call(
        kern,
        out_shape=[jax.ShapeDtypeStruct((nb, lc, wd), BF16), jax.ShapeDtypeStruct((nb, lx, wd), BF16)],
        grid=(nb,),
        in_specs=in_specs,
        out_specs=[seq(lc, wd), seq(lx, wd)],
        scratch_shapes=[pltpu.VMEM((wd, wd), F32), pltpu.VMEM((wd, wd), F32),
                        pltpu.VMEM((lc, wd), F32), pltpu.VMEM((lc, wd), F32),
                        pltpu.VMEM((lx, wd), F32), pltpu.VMEM((lx, wd), F32)],
        compiler_params=_params("parallel"),
        name="delta_rule",
    )(qc, kc, vc, bac, zc, qx, kx, vx, bax, zx, alog, dtb, gn, e_mean, sel)


def _attn_kernel(q_ref, k_ref, vt_ref, o_ref, qs_ref, s_ref, p_ref, al_ref, m_ref, acc_ref, *, tq, tk, n_k):
    for h in range(ATT_GROUP):
        qs_ref[h * tq:(h + 1) * tq, :] = q_ref[0, :, h * ATT_HD:(h + 1) * ATT_HD]
    m_ref[...] = jnp.full(m_ref.shape, -1e30, F32)
    acc_ref[...] = jnp.zeros(acc_ref.shape, F32)

    def scores(j, slot):
        s_ref[slot] = _dot_nt(k_ref[0, 0, j * tk:(j + 1) * tk, :], qs_ref[...])

    def softmax(slot):
        s = s_ref[slot]
        m_old = m_ref[...]
        m_new = jnp.maximum(m_old, jnp.max(s, axis=0, keepdims=True))
        p_ref[slot] = jnp.exp2(s - m_new).astype(BF16)
        al_ref[slot] = jnp.exp2(m_old - m_new)
        m_ref[...] = m_new

    def values(j, slot):
        acc_ref[...] = acc_ref[...] * al_ref[slot] + _dot(vt_ref[0, 0, :, j * tk:(j + 1) * tk], p_ref[slot])

    for i in range(n_k + 2):
        if i < n_k:
            scores(i, i % 2)
        if 1 <= i <= n_k:
            softmax((i - 1) % 2)
        if i >= 2:
            values(i - 2, i % 2)

    acc = acc_ref[...]
    out = acc[:ATT_HD] / acc[ATT_HD:ATT_HD + 1]
    o_ref[0] = jnp.concatenate([out[:, h * tq:(h + 1) * tq].T for h in range(ATT_GROUP)], axis=1).astype(BF16)


def _attn_call(q, k4, vt):
    nb, lq, _ = q.shape
    s = k4.shape[2]
    tq = _tile(lq, 512)
    tk = 256
    gw = ATT_GROUP * ATT_HD
    cols = ATT_GROUP * tq
    kern = functools.partial(_attn_kernel, tq=tq, tk=tk, n_k=s // tk)
    return pl.pallas_call(
        kern,
        out_shape=jax.ShapeDtypeStruct((nb, lq, Q_W), BF16),
        grid=(nb, ATT_KV, lq // tq),
        in_specs=[
            pl.BlockSpec((1, tq, gw), lambda b, g, i: (b, i, g)),
            pl.BlockSpec((1, 1, s, ATT_HD), lambda b, g, i: (b, g, 0, 0)),
            pl.BlockSpec((1, 1, V_ROWS, s), lambda b, g, i: (b, g, 0, 0)),
        ],
        out_specs=pl.BlockSpec((1, tq, gw), lambda b, g, i: (b, i, g)),
        scratch_shapes=[pltpu.VMEM((cols, ATT_HD), BF16), pltpu.VMEM((2, tk, cols), F32),
                        pltpu.VMEM((2, tk, cols), BF16), pltpu.VMEM((2, 1, cols), F32),
                        pltpu.VMEM((1, cols), F32), pltpu.VMEM((V_ROWS, cols), F32)],
        compiler_params=_params("parallel", "parallel", "arbitrary"),
        name="gqa_attn",
    )(q, k4, vt)


def _merge_kernel(pp_ref, pc_ref, pn_ref, x_ref, mod_ref, yd_ref, ya_ref, gate_ref, pw_ref, ps_ref,
                  wb_ref, wo_ref, o_ref, pad_ref, *, tm, seq):
    d = x_ref.shape[2]
    a0 = POOL_WIDTH
    a1 = POOL_WIDTH + DN_VW
    _fill_halo(pad_ref, pp_ref, pc_ref, pn_ref, tm)
    pooled = _window_means_minus_self(pad_ref, tm, seq)
    y_pool = (_dot(pooled.astype(BF16), pw_ref[...]) * ps_ref[...]).astype(BF16)
    m = gate_ref[0, :, 0:d].astype(F32) * _dot(y_pool, wb_ref[0:a0, :])
    m = m + gate_ref[0, :, d:2 * d].astype(F32) * _dot(yd_ref[0], wb_ref[a0:a1, :])
    m = m + gate_ref[0, :, 2 * d:3 * d].astype(F32) * _dot(ya_ref[0], wb_ref[a1:, :])
    o_ref[0] = x_ref[0] + mod_ref[0, 2] * _dot(m.astype(BF16), wo_ref[...])


def _merge_call(x, mod, sub, pool, y_dn, y_att, gate, pool_bd, pool_s, w_branch, w_out):
    nb, t, d = x.shape
    tm = _tile(t, 512)
    row = lambda w: pl.BlockSpec((1, tm, w), lambda b, i: (b, i, 0))
    kern = functools.partial(_merge_kernel, tm=tm, seq=t)
    return pl.pallas_call(
        kern,
        out_shape=jax.ShapeDtypeStruct(x.shape, F32),
        grid=(nb, t // tm),
        in_specs=_halo_specs(tm, POOL_WIDTH, t) + [
            row(d), _mod_spec(mod, nb, sub, d), row(DN_VW), row(Q_W), row(N_BRANCH * d),
            _resident(pool_bd.shape), _resident(pool_s.shape),
            _resident(w_branch.shape), _resident(w_out.shape)],
        out_specs=row(d),
        scratch_shapes=[pltpu.VMEM((tm + 2 * SUBLANE, POOL_WIDTH), F32)],
        compiler_params=_params("parallel", "arbitrary"),
        name="merge",
    )(pool, pool, pool, x, mod, y_dn, y_att, gate, pool_bd, pool_s, w_branch, w_out)


def _final_norm_kernel(x_ref, g_ref, o_ref):
    x = x_ref[0]
    ms = jnp.mean(x * x, axis=-1, keepdims=True)
    o_ref[0] = x * lax.rsqrt(ms + EPS) * g_ref[...]


def _final_norm_call(x, gain):
    nb, t, d = x.shape
    tm = _tile(t, 512)
    return pl.pallas_call(
        _final_norm_kernel,
        out_shape=jax.ShapeDtypeStruct(x.shape, F32),
        grid=(nb, t // tm),
        in_specs=[pl.BlockSpec((1, tm, d), lambda b, i: (b, i, 0)), _resident((1, d))],
        out_specs=pl.BlockSpec((1, tm, d), lambda b, i: (b, i, 0)),
        compiler_params=_params("parallel", "parallel"),
        name="final_norm",
    )(x, gain.reshape(1, d))


def _block_diag_const(width, block, value):
    idx = np.arange(width)
    return jnp.asarray(np.where((idx[:, None] // block) == (idx[None, :] // block), value, 0.0), BF16)


def _pair_swap_const(width):
    idx = np.arange(width)
    return jnp.asarray((idx[:, None] == (idx[None, :] ^ 1)).astype(np.float32), BF16)


def _rope_tables(n_tokens):
    rows = n_tokens // GRID_W
    row = jnp.broadcast_to(jnp.arange(rows)[:, None], (rows, GRID_W)).reshape(-1)
    col = jnp.broadcast_to(jnp.arange(GRID_W)[None, :], (rows, GRID_W)).reshape(-1)
    n_freq = ATT_HD // 4
    inv = ROPE_BASE ** (-jnp.arange(n_freq, dtype=F32) / n_freq)
    ang = jnp.concatenate([row[:, None].astype(F32) * inv, col[:, None].astype(F32) * inv], axis=-1)
    cos = jnp.repeat(jnp.cos(ang), 2, axis=1)
    sin = jnp.repeat(jnp.sin(ang), 2, axis=1)
    sign = jnp.where(jnp.arange(ATT_HD) % 2 == 0, -1.0, 1.0).astype(F32)
    sin = sin * sign
    reps = LANE // ATT_HD
    return jnp.tile(cos, (1, reps)), jnp.tile(sin, (1, reps))


def _permute_w_in(w_in):
    d = w_in.shape[0]
    o_pool = 0
    o_qkv = o_pool + POOL_WIDTH
    o_z = o_qkv + DN_CONV_CH
    o_ba = o_z + DN_VW
    o_aq = o_ba + 4 * DN_HEADS
    o_gate = o_aq + Q_W + 2 * KV_W
    pad = jnp.zeros((d, BA_PAD - 4 * DN_HEADS), w_in.dtype)
    return jnp.concatenate([w_in[:, o_pool:o_ba], w_in[:, o_aq:], w_in[:, o_ba:o_aq], pad], axis=1).astype(BF16)


def _head_select_const():
    sel = np.zeros((2, LANE, 2 * DN_VW), np.float32)
    for d in range(2):
        for h in range(DN_HEADS):
            sel[d, d * DN_HEADS + h, h * DN_DK:(h + 1) * DN_DK] = 1.0
            sel[d, 2 * DN_HEADS + d * DN_HEADS + h, DN_VW + h * DN_DK:DN_VW + (h + 1) * DN_DK] = 1.0
    return jnp.asarray(sel, BF16)


def _lane_vec(vals, offset):
    out = jnp.zeros((1, LANE), F32)
    return out.at[0, offset:offset + vals.size].set(vals.reshape(-1).astype(F32))


def _kv_layout(k_c, v_c, k_x=None, v_x=None):
    k = k_c if k_x is None else jnp.concatenate([k_c, k_x], axis=1)
    v = v_c if v_x is None else jnp.concatenate([v_c, v_x], axis=1)
    nb, s, _ = k.shape
    k4 = k.reshape(nb, s, ATT_KV, ATT_HD).transpose(0, 2, 1, 3)
    vt = v.reshape(nb, s, ATT_KV, ATT_HD).transpose(0, 2, 3, 1)
    ones = jnp.ones((nb, ATT_KV, 1, s), vt.dtype)
    zeros = jnp.zeros((nb, ATT_KV, V_ROWS - ATT_HD - 1, s), vt.dtype)
    return k4, jnp.concatenate([vt, ones, zeros], axis=2)


def kernel(x, c, ctx, c_ctx, w_ada, b_ada, norm_ffn1, w_ffn1_in, w_ffn1_out, norm_mix, w_in, pool_w, pool_scale, dn_conv, dn_a_log, dn_dt_bias, dn_norm, q_norm, k_norm, w_branch, w_out, norm_ffn2, w_ffn2_in, w_ffn2_out, norm_final):
    nb, seq, d = x.shape
    lc = ctx.shape[1]
    depth = w_ada.shape[0]

    n_rows = -(-(nb + 1) // SUBLANE) * SUBLANE
    cond = jnp.concatenate([c, c_ctx[None], jnp.zeros((n_rows - nb - 1, d), F32)], axis=0)
    mod_all = _ada_call(cond, w_ada, b_ada).reshape(depth, n_rows, N_MOD, 1, d)

    cos_x, sin_x = _rope_tables(seq)
    cos_c = jnp.ones((lc, LANE), F32)
    sin_c = jnp.zeros((lc, LANE), F32)
    e_q = _block_diag_const(Q_W, ATT_HD, 1.0 / ATT_HD)
    e_k = _block_diag_const(KV_W, ATT_HD, 1.0 / ATT_HD)
    p_q = _pair_swap_const(Q_W)
    p_k = _pair_swap_const(KV_W)
    e_sum = _block_diag_const(DN_QK, DN_DK, 1.0)
    e_mean = _block_diag_const(DN_VW, DN_DV, 1.0 / DN_DV)
    head_sel = _head_select_const()

    xc = ctx.reshape(1, nb * lc, d)
    for l in range(depth):
        ctx_out = l < depth - 1
        mod_x = mod_all[l, :nb]
        mod_c = mod_all[l, nb:nb + 1]
        wf1_in = w_ffn1_in[l].astype(BF16)
        wf1_out = w_ffn1_out[l].astype(BF16)
        wf2_in = w_ffn2_in[l].astype(BF16)
        wf2_out = w_ffn2_out[l].astype(BF16)
        w_perm = _permute_w_in(w_in[l])
        w_br = w_branch[l].astype(BF16)
        w_o = w_out[l].astype(BF16)
        qg = jnp.tile(q_norm[l], ATT_HEADS).reshape(1, Q_W)
        kg = jnp.tile(k_norm[l], ATT_KV).reshape(1, KV_W)
        pool_bd = jax.scipy.linalg.block_diag(*[pool_w[l, g] for g in range(len(POOL_WINDOWS))]).astype(BF16)
        pool_s = pool_scale[l].reshape(1, POOL_WIDTH)
        alog = _lane_vec(dn_a_log[l], 2 * DN_HEADS)
        dtb = _lane_vec(dn_dt_bias[l], 2 * DN_HEADS)
        gn = jnp.tile(dn_norm[l], DN_HEADS).reshape(1, DN_VW)

        x = _ffn_call(x, mod_x, 0, norm_ffn1[l], wf1_in, wf1_out)
        xc = _ffn_call(xc, mod_c, 0, norm_ffn1[l], wf1_in, wf1_out)

        proj_consts = (qg, kg, e_q, p_q, e_k, p_k, dn_conv[l], e_sum)
        xc_seq = xc.reshape(nb, lc, d)
        pool_x, qd_x, kd_x, vd_x, z_x, ba_x, aq_x, ak_x, av_x, gate_x = _proj_call(
            x, mod_x, 1, norm_mix[l], w_perm, cos_x, sin_x, *proj_consts)
        pool_c, qd_c, kd_c, vd_c, z_c, ba_c, aq_c, ak_c, av_c, gate_c = _proj_call(
            xc_seq, mod_c, 1, norm_mix[l], w_perm, cos_c, sin_c, *proj_consts)

        yd_c, yd_x = _delta_call((qd_c, kd_c, vd_c, ba_c, z_c), (qd_x, kd_x, vd_x, ba_x, z_x),
                                 alog, dtb, gn, e_mean, head_sel)

        k4_all, vt_all = _kv_layout(ak_c, av_c, ak_x, av_x)
        ya_x = _attn_call(aq_x, k4_all, vt_all)
        x = _merge_call(x, mod_x, 1, pool_x, yd_x, ya_x, gate_x, pool_bd, pool_s, w_br, w_o)

        if ctx_out:
            k4_c, vt_c = _kv_layout(ak_c, av_c)
            ya_c = _attn_call(aq_c, k4_c, vt_c)
            xc = _merge_call(xc_seq, mod_c, 1, pool_c, yd_c, ya_c, gate_c, pool_bd, pool_s, w_br, w_o)
            xc = xc.reshape(1, nb * lc, d)

        x = _ffn_call(x, mod_x, 2, norm_ffn2[l], wf2_in, wf2_out)
        if ctx_out:
            xc = _ffn_call(xc, mod_c, 2, norm_ffn2[l], wf2_in, wf2_out)

    return _final_norm_call(x, norm_final)
```

```python
import functools
import math

import jax
import jax.numpy as jnp
import numpy as np
from jax import lax
from jax.experimental import pallas as pl
from jax.experimental.pallas import tpu as pltpu

F32 = jnp.float32
BF16 = jnp.bfloat16

N_MOD = 9
FFN_RES = 0.5
POOL_WIDTH = 256
POOL_WINDOWS = (2, 4, 8, 16)
POOL_GROUP = 64
DN_HEADS = 4
DN_DK = 64
DN_DV = 64
DN_CONV = 5
DN_CHUNK = 64
DN_QK = DN_HEADS * DN_DK
DN_VW = DN_HEADS * DN_DV
DN_CONV_CH = 2 * DN_QK + DN_VW
ATT_HEADS = 8
ATT_KV = 2
ATT_GROUP = ATT_HEADS // ATT_KV
ATT_HD = 64
ROPE_BASE = 10000.0
GRID_W = 64
N_BRANCH = 3
EPS = 1e-6
LOG2E = 1.4426950408889634

LANE = 128
SUBLANE = 8
VMEM_LIMIT = 56 * 1024 * 1024

BA_PAD = LANE
Q_W = ATT_HEADS * ATT_HD
KV_W = ATT_KV * ATT_HD
V_ROWS = ATT_HD + 16
C_POOL = 0
C_QKV = C_POOL + POOL_WIDTH
C_Z = C_QKV + DN_CONV_CH
C_AQ = C_Z + DN_VW
C_AK = C_AQ + Q_W
C_AV = C_AK + KV_W
C_GATE = C_AV + KV_W


def _params(*sem):
    return pltpu.CompilerParams(dimension_semantics=sem, vmem_limit_bytes=VMEM_LIMIT)


def _tile(n, target):
    t = min(target, n)
    assert n % t == 0 and t % SUBLANE == 0, (n, t)
    return t


def _resident(shape):
    nd = len(shape)
    return pl.BlockSpec(shape, lambda *_: (0,) * nd, pipeline_mode=pl.Buffered(1))


def _silu(x):
    return x * jax.nn.sigmoid(x)


def _dot(a, b):
    return jnp.dot(a, b, preferred_element_type=F32)


def _dot_nt(a, b):
    return lax.dot_general(a, b, (((1,), (1,)), ((), ())), preferred_element_type=F32)


def _dot_tn(a, b):
    return lax.dot_general(a, b, (((0,), (0,)), ((), ())), preferred_element_type=F32)


def _split3(x):
    hi = x.astype(BF16)
    r1 = x - hi.astype(F32)
    mid = r1.astype(BF16)
    lo = (r1 - mid.astype(F32)).astype(BF16)
    return hi, mid, lo


def _dot_exact_lhs(a_bf16, x_f32):
    hi, mid, lo = _split3(x_f32)
    return _dot(a_bf16, hi) + _dot(a_bf16, mid) + _dot(a_bf16, lo)


def _dot_exact_rhs(x_f32, b_bf16):
    hi, mid, lo = _split3(x_f32)
    return _dot(hi, b_bf16) + _dot(mid, b_bf16) + _dot(lo, b_bf16)


def _ada_kernel(s_ref, w_ref, b_ref, o_ref):
    s = _silu(s_ref[...])
    w = w_ref[0]
    s_hi, s_mid, s_lo = _split3(s)
    w_hi, w_mid, w_lo = _split3(w)
    acc = _dot(s_hi, w_hi)
    acc = acc + _dot(s_hi, w_mid) + _dot(s_mid, w_hi)
    acc = acc + _dot(s_hi, w_lo) + _dot(s_mid, w_mid) + _dot(s_lo, w_hi)
    o_ref[0] = acc + b_ref[0]


def _ada_call(cond, w_ada, b_ada):
    depth, d, n = w_ada.shape
    rows = cond.shape[0]
    tn = 1024
    return pl.pallas_call(
        _ada_kernel,
        out_shape=jax.ShapeDtypeStruct((depth, rows, n), F32),
        grid=(depth, n // tn),
        in_specs=[
            pl.BlockSpec((rows, d), lambda l, j: (0, 0)),
            pl.BlockSpec((1, d, tn), lambda l, j: (l, 0, j)),
            pl.BlockSpec((1, 1, tn), lambda l, j: (l, 0, j)),
        ],
        out_specs=pl.BlockSpec((1, rows, tn), lambda l, j: (l, 0, j)),
        compiler_params=_params("parallel", "parallel"),
        name="ada_mod",
    )(cond, w_ada, b_ada.reshape(depth, 1, n))


def _norm_mod(x, gain, shift, scale):
    ms = jnp.mean(x * x, axis=-1, keepdims=True)
    h = x * lax.rsqrt(ms + EPS) * gain
    return h * (1.0 + scale) + shift


def _ffn_kernel(x_ref, mod_ref, g_ref, wi_ref, wo_ref, o_ref, acc_ref, *, d_ff, tf):
    x = x_ref[0]
    h = _norm_mod(x, g_ref[...], mod_ref[0, 0], mod_ref[0, 1]).astype(BF16)
    for j in range(d_ff // tf):
        g = _dot(h, wi_ref[:, j * tf:(j + 1) * tf])
        u = _dot(h, wi_ref[:, d_ff + j * tf:d_ff + (j + 1) * tf])
        a = (_silu(g) * u).astype(BF16)
        part = _dot(a, wo_ref[j * tf:(j + 1) * tf, :])
        if j == 0:
            acc_ref[...] = part
        else:
            acc_ref[...] += part
    o_ref[0] = x + (FFN_RES * mod_ref[0, 2]) * acc_ref[...]


def _ffn_call(x, mod, sub, gain, w_in, w_out):
    nb, t, d = x.shape
    d_ff = w_out.shape[0]
    tm = _tile(t, 512)
    tf = 256
    kern = functools.partial(_ffn_kernel, d_ff=d_ff, tf=tf)
    return pl.pallas_call(
        kern,
        out_shape=jax.ShapeDtypeStruct(x.shape, F32),
        grid=(nb, t // tm),
        in_specs=[
            pl.BlockSpec((1, tm, d), lambda b, i: (b, i, 0)),
            pl.BlockSpec((1, 3, 1, d), lambda b, i: (b, sub, 0, 0)),
            _resident((1, d)),
            _resident((d, 2 * d_ff)),
            _resident((d_ff, d)),
        ],
        out_specs=pl.BlockSpec((1, tm, d), lambda b, i: (b, i, 0)),
        scratch_shapes=[pltpu.VMEM((tm, d), F32)],
        compiler_params=_params("parallel", "parallel"),
        name="ffn",
    )(x, mod, gain.reshape(1, d), w_in, w_out)


def _head_norm_rope(t_raw, gain, e_mat, p_mat, cos, sin, out_scale):
    ms = _dot((t_raw * t_raw).astype(BF16), e_mat)
    r = lax.rsqrt(ms + EPS) * out_scale
    t = t_raw * gain
    t_sw = _dot(t.astype(BF16), p_mat)
    w = t_raw.shape[1]
    outs = []
    for g in range(w // LANE):
        sl = slice(g * LANE, (g + 1) * LANE)
        outs.append(r[:, sl] * (t[:, sl] * cos + t_sw[:, sl] * sin))
    return jnp.concatenate(outs, axis=1) if len(outs) > 1 else outs[0]


def _proj_kernel(xp_ref, x_ref, xn_ref, mod_ref, g_ref, w_ref, cos_ref, sin_ref, qg_ref, kg_ref,
                 eq_ref, pq_ref, ek_ref, pk_ref, cw_ref, es_ref,
                 pool_ref, dq_ref, dk_ref, dv_ref, z_ref, ba_ref, q_ref, k_ref, v_ref, gate_ref,
                 pad_ref, *, n_gate, tm):
    gain = g_ref[...]
    shift = mod_ref[0, 0]
    scale = mod_ref[0, 1]
    h = _norm_mod(x_ref[0], gain, shift, scale).astype(BF16)

    def cols(lo, width):
        return _dot(h, w_ref[:, lo:lo + width])

    pool_ref[0] = cols(C_POOL, POOL_WIDTH)

    i = pl.program_id(1)
    n = pl.num_programs(1)
    halo = jnp.concatenate([xp_ref[0], xn_ref[0]], axis=0)
    h_halo = _norm_mod(halo, gain, shift, scale).astype(BF16)
    w_qkv = w_ref[:, C_QKV:C_QKV + DN_CONV_CH]
    qkv_halo = _dot(h_halo, w_qkv)
    pad_ref[0:SUBLANE, :] = jnp.where(i > 0, qkv_halo[0:SUBLANE], 0.0)
    pad_ref[SUBLANE + tm:2 * SUBLANE + tm, :] = jnp.where(i < n - 1, qkv_halo[SUBLANE:2 * SUBLANE], 0.0)
    for j in range(DN_CONV_CH // 256):
        pad_ref[SUBLANE:SUBLANE + tm, j * 256:(j + 1) * 256] = cols(C_QKV + j * 256, 256)
    z_ref[0] = cols(C_Z, DN_VW).astype(BF16)
    q_raw = cols(C_AQ, Q_W)
    k_raw = cols(C_AK, KV_W)
    v_ref[0] = cols(C_AV, KV_W).astype(BF16)
    tg = 512
    for j in range(n_gate // tg):
        gate_ref[0, :, j * tg:(j + 1) * tg] = jax.nn.sigmoid(cols(C_GATE + j * tg, tg)).astype(BF16)
    ba_ref[0] = cols(C_GATE + n_gate, BA_PAD)

    cos = cos_ref[...]
    sin = sin_ref[...]
    q = _head_norm_rope(q_raw, qg_ref[...], eq_ref[...], pq_ref[...], cos, sin, (ATT_HD ** -0.5) * LOG2E)
    q_ref[0] = q.astype(BF16)
    k = _head_norm_rope(k_raw, kg_ref[...], ek_ref[...], pk_ref[...], cos, sin, 1.0)
    k_ref[0] = k.astype(BF16)

    half = DN_CONV // 2
    n_pad = tm + 2 * SUBLANE
    for part, o_ref in enumerate((dq_ref, dk_ref, dv_ref)):
        cs = slice(part * DN_QK, (part + 1) * DN_QK)
        xp = pad_ref[:, cs]
        acc = None
        for j in range(DN_CONV):
            shifted = xp if j == half else pltpu.roll(xp, (half - j) % n_pad, 0)
            term = shifted[SUBLANE:SUBLANE + tm] * cw_ref[j:j + 1, cs]
            acc = term if acc is None else acc + term
        y = _silu(acc)
        if part < 2:
            y = y * lax.rsqrt(_dot((y * y).astype(BF16), es_ref[...]) + EPS)
            if part == 0:
                y = y * (DN_DK ** -0.5)
        o_ref[0] = y.astype(BF16)


def _mod_spec(mod, nb, sub, d):
    if mod.shape[0] == nb:
        return pl.BlockSpec((1, 3, 1, d), lambda b, i: (b, sub, 0, 0))
    return pl.BlockSpec((1, 3, 1, d), lambda b, i: (0, sub, 0, 0))


def _proj_call(x, mod, sub, gain, w_perm, cos, sin, qg, kg, eq, pq, ek, pk, conv_w, e_sum):
    nb, t, d = x.shape
    n_gate = N_BRANCH * d
    ncol = w_perm.shape[1]
    tm = _tile(t, 512)
    nt = t // tm
    row = lambda w: pl.BlockSpec((1, tm, w), lambda b, i: (b, i, 0))
    widths = [(POOL_WIDTH, F32), (DN_QK, BF16), (DN_QK, BF16), (DN_VW, BF16), (DN_VW, BF16), (BA_PAD, F32),
              (Q_W, BF16), (KV_W, BF16), (KV_W, BF16), (n_gate, BF16)]
    kern = functools.partial(_proj_kernel, n_gate=n_gate, tm=tm)
    return pl.pallas_call(
        kern,
        out_shape=[jax.ShapeDtypeStruct((nb, t, wd), dt) for wd, dt in widths],
        grid=(nb, nt),
        in_specs=_halo_specs(tm, d, t) + [
            _mod_spec(mod, nb, sub, d),
            _resident((1, d)),
            _resident((d, ncol)),
            pl.BlockSpec((tm, LANE), lambda b, i: (i, 0)),
            pl.BlockSpec((tm, LANE), lambda b, i: (i, 0)),
            _resident((1, Q_W)),
            _resident((1, KV_W)),
            _resident((Q_W, Q_W)),
            _resident((Q_W, Q_W)),
            _resident((KV_W, KV_W)),
            _resident((KV_W, KV_W)),
            _resident(conv_w.shape),
            _resident(e_sum.shape),
        ],
        out_specs=[row(wd) for wd, _ in widths],
        scratch_shapes=[pltpu.VMEM((tm + 2 * SUBLANE, DN_CONV_CH), F32)],
        compiler_params=_params("parallel", "arbitrary"),
        name="mix_proj",
    )(x, x, x, mod, gain.reshape(1, d), w_perm, cos, sin, qg, kg, eq, pq, ek, pk, conv_w, e_sum)


def _fill_halo(pad_ref, prev_ref, cur_ref, next_ref, tt):
    i = pl.program_id(1)
    n = pl.num_programs(1)
    prev = prev_ref[0]
    nxt = next_ref[0]
    pad_ref[0:SUBLANE, :] = jnp.where(i > 0, prev, jnp.zeros_like(prev))
    pad_ref[SUBLANE:SUBLANE + tt, :] = cur_ref[0]
    pad_ref[SUBLANE + tt:2 * SUBLANE + tt, :] = jnp.where(i < n - 1, nxt, jnp.zeros_like(nxt))


def _halo_specs(tt, width, n_rows):
    per = tt // SUBLANE
    last = n_rows // SUBLANE - 1
    return [
        pl.BlockSpec((1, SUBLANE, width), lambda b, i: (b, jnp.maximum(i * per - 1, 0), 0)),
        pl.BlockSpec((1, tt, width), lambda b, i: (b, i, 0)),
        pl.BlockSpec((1, SUBLANE, width), lambda b, i: (b, jnp.minimum((i + 1) * per, last), 0)),
    ]


def _window_means_minus_self(pad_ref, tt, seq):
    assert POOL_WINDOWS == (2, 4, 8, 16) and POOL_GROUP * 2 == LANE
    i = pl.program_id(1)
    t = i * tt + lax.broadcasted_iota(jnp.int32, (tt, 1), 0)
    lane = lax.broadcasted_iota(jnp.int32, (1, LANE), 1)
    n = tt + 2 * SUBLANE
    body = slice(SUBLANE, SUBLANE + tt)

    def count(w):
        return (jnp.minimum(t + w - w // 2, seq) - jnp.maximum(t - w // 2, 0)).astype(F32)

    def ahead(a, k):
        return pltpu.roll(a, n - k, 0)

    def behind(a, k):
        return pltpu.roll(a, k, 0)

    tiles = []
    for lt in range(POOL_WIDTH // LANE):
        x = pad_ref[:, lt * LANE:(lt + 1) * LANE]
        r2 = x + ahead(x, 1)
        r4 = r2 + ahead(r2, 2)
        if lt == 0:
            narrow = behind(r2, 1)[body] / count(2)
            wide = behind(r4, 2)[body] / count(4)
        else:
            r8 = r4 + ahead(r4, 4)
            narrow = behind(r8, 4)[body] / count(8)
            wide = (r8[0:tt] + r8[body]) / count(16)
        tiles.append(jnp.where(lane >= POOL_GROUP, wide, narrow) - x[body])
    return jnp.concatenate(tiles, axis=1)


def _delta_kernel(qc_ref, kc_ref, vc_ref, bac_ref, zc_ref, qx_ref, kx_ref, vx_ref, bax_ref, zx_ref,
                  alog_ref, dtb_ref, gn_ref, e_ref, sel_ref, yc_ref, yx_ref,
                  sf_ref, sb_ref, ocf_ref, ocb_ref, oxf_ref, oxb_ref, *, n_ctx, n_lat, group):
    c = DN_CHUNK
    w = DN_HEADS * c
    row = lax.broadcasted_iota(jnp.int32, (w, w), 0)
    col = lax.broadcasted_iota(jnp.int32, (w, w), 1)
    rc_xor = row ^ col
    same = (rc_xor >> 6) == 0
    rp = row & (c - 1)
    cp = col & (c - 1)
    eye = jnp.where(row == col, 1.0, 0.0).astype(F32)
    m_incl = (same & (rp >= cp), same & (rp <= cp))
    m_strict = (same & (rp > cp), same & (rp < cp))
    r64 = lax.broadcasted_iota(jnp.int32, (c, c), 0)
    c64 = lax.broadcasted_iota(jnp.int32, (c, c), 1)
    cum_mat = (jnp.where(r64 >= c64, 1.0, 0.0).astype(BF16), jnp.where(r64 <= c64, 1.0, 0.0).astype(BF16))
    neg_a = -jnp.exp(alog_ref[...])
    dtb = dtb_ref[...]

    sf_ref[...] = jnp.zeros_like(sf_ref)
    sb_ref[...] = jnp.zeros_like(sb_ref)

    def spread(a):
        return jnp.where(same, jnp.concatenate([a, a, a, a], axis=0), 0.0)

    def gather_heads(o):
        return o[0:c] + o[c:2 * c] + o[2 * c:3 * c] + o[3 * c:4 * c]

    def prepare(refs, chunks):
        q_ref, k_ref, v_ref, ba_ref = refs
        st = []
        for rows, d in chunks:
            q = q_ref[0, rows, :].astype(F32)
            k = k_ref[0, rows, :].astype(F32)
            v = v_ref[0, rows, :].astype(F32)
            ba = ba_ref[0, rows, :]
            g_all = neg_a * jax.nn.softplus(ba + dtb)
            cum = _dot_exact_lhs(cum_mat[d], g_all)
            st.append(dict(d=d, q=q, k=k, v=v, ba=ba, cum=cum))
        for x in st:
            sel = sel_ref[x["d"]]
            x["beta"] = _dot(jax.nn.sigmoid(x["ba"]).astype(BF16), sel[:, :w])
            x["g_cum"] = _dot_exact_rhs(x["cum"], sel[:, w:])
        for x in st:
            x["kb"] = x["k"] * x["beta"]
            k_bd = spread(x["k"]).astype(BF16)
            x["prod"] = _dot_nt(jnp.concatenate([spread(x["kb"]), spread(x["q"])], axis=0).astype(BF16), k_bd)
        for x in st:
            d = x["d"]
            g_col = spread(x["g_cum"])
            decay = jnp.where(m_incl[d], jnp.exp(g_col - g_col.T), 0.0)
            a = jnp.where(m_strict[d], x["prod"][:w] * decay, 0.0)
            x["qkd"] = (x["prod"][w:] * decay).astype(BF16)
            x["a"] = a
            a4 = jnp.where((rc_xor >> 2) == 0, a, 0.0)
            x["t0"] = eye - a4
            x["a4"] = a4.astype(BF16)
        for x in st:
            x["a4sq"] = _dot(x["a4"], x["a4"]).astype(BF16)
        for x in st:
            x["t"] = x["t0"] + _dot(x["t0"].astype(BF16), x["a4sq"])
        for lg in range(2, 6):
            for x in st:
                b_ring = jnp.where((rc_xor >> lg) == 1, x["a"], 0.0).astype(BF16)
                x["t_b"] = x["t"].astype(BF16)
                x["tb"] = _dot(x["t_b"], b_ring).astype(BF16)
            for x in st:
                x["t"] = x["t"] - _dot(x["tb"], x["t_b"])
        out = []
        for x in st:
            last = c - 1 if x["d"] == 0 else 0
            g_cum = x["g_cum"]
            g_tot = g_cum[last:last + 1, :]
            e_g = jnp.exp(g_cum)
            rhs = jnp.concatenate([spread(x["v"] * x["beta"]), spread(x["kb"] * e_g)], axis=1).astype(BF16)
            uw = _dot(x["t"].astype(BF16), rhs)
            out.append(dict(u=uw[:, :w],
                            lhs=jnp.concatenate([uw[:, w:], spread(x["q"] * e_g)], axis=0).astype(BF16),
                            qkd=x["qkd"], kd=spread(x["k"] * jnp.exp(g_tot - g_cum)).astype(BF16),
                            e_tot=jnp.exp(g_tot)))
        return out

    def advance(pre, s_ref):
        s_old = s_ref[...]
        ws_qs = _dot(pre["lhs"], s_old.astype(BF16))
        v_new = (pre["u"] - ws_qs[:w]).astype(BF16)
        o = ws_qs[w:] + _dot(pre["qkd"], v_new)
        s_ref[...] = s_old * pre["e_tot"] + _dot_tn(pre["kd"], v_new)
        return gather_heads(o)

    def run_block(refs, of_ref, ob_ref, rows_f, rows_b):
        chunks = []
        for g in range(len(rows_f)):
            chunks += [(rows_f[g], 0), (rows_b[g], 1)]
        pre = prepare(refs, chunks)
        for g in range(len(rows_f)):
            of_ref[rows_f[g], :] = advance(pre[2 * g], sf_ref)
            ob_ref[rows_b[g], :] = advance(pre[2 * g + 1], sb_ref)

    ctx_refs = (qc_ref, kc_ref, vc_ref, bac_ref)
    lat_refs = (qx_ref, kx_ref, vx_ref, bax_ref)
    for j in range(n_ctx // group):
        run_block(ctx_refs, ocf_ref, ocb_ref,
                  [pl.ds((j * group + g) * c, c) for g in range(group)],
                  [pl.ds((n_ctx - 1 - j * group - g) * c, c) for g in range(group)])

    def lat_body(j, carry):
        rows_f = [pl.ds(pl.multiple_of((j * group + g) * c, c), c) for g in range(group)]
        rows_b = [pl.ds(pl.multiple_of((n_lat - 1 - j * group - g) * c, c), c) for g in range(group)]
        run_block(lat_refs, oxf_ref, oxb_ref, rows_f, rows_b)
        return carry

    lax.fori_loop(0, n_lat // group, lat_body, 0)

    def finish(of_ref, ob_ref, z_ref, y_ref, n_rows):
        tt = min(512, n_rows)
        for j in range(n_rows // tt):
            rs = slice(j * tt, (j + 1) * tt)
            o = of_ref[rs, :] + ob_ref[rs, :]
            ms = _dot((o * o).astype(BF16), e_ref[...])
            y = o * lax.rsqrt(ms + EPS) * gn_ref[...]
            y_ref[0, rs, :] = (y * _silu(z_ref[0, rs, :].astype(F32))).astype(BF16)

    finish(ocf_ref, ocb_ref, zc_ref, yc_ref, n_ctx * c)
    finish(oxf_ref, oxb_ref, zx_ref, yx_ref, n_lat * c)


def _delta_call(ctx_parts, lat_parts, alog, dtb, gn, e_mean, sel):
    qc, kc, vc, bac, zc = ctx_parts
    qx, kx, vx, bax, zx = lat_parts
    nb, lc, wd = qc.shape
    lx = qx.shape[1]
    group = 2
    assert lc % (group * DN_CHUNK) == 0 and lx % (group * DN_CHUNK) == 0, (lc, lx)
    assert lc % _tile(lc, 512) == 0 and lx % _tile(lx, 512) == 0
    kern = functools.partial(_delta_kernel, n_ctx=lc // DN_CHUNK, n_lat=lx // DN_CHUNK, group=group)
    seq = lambda n, width: pl.BlockSpec((1, n, width), lambda b: (b, 0, 0))
    in_specs = [seq(lc, wd), seq(lc, wd), seq(lc, wd), seq(lc, BA_PAD), seq(lc, wd),
                seq(lx, wd), seq(lx, wd), seq(lx, wd), seq(lx, BA_PAD), seq(lx, wd),
                _resident((1, LANE)), _resident((1, LANE)), _resident((1, wd)), _resident((wd, wd)),
                _resident(sel.shape)]
    return pl.pallas_call(
        kern,
        out_shape=[jax.ShapeDtypeStruct((nb, lc, wd), BF16), jax.ShapeDtypeStruct((nb, lx, wd), BF16)],
        grid=(nb,),
        in_specs=in_specs,
        out_specs=[seq(lc, wd), seq(lx, wd)],
        scratch_shapes=[pltpu.VMEM((wd, wd), F32), pltpu.VMEM((wd, wd), F32),
                        pltpu.VMEM((lc, wd), F32), pltpu.VMEM((lc, wd), F32),
                        pltpu.VMEM((lx, wd), F32), pltpu.VMEM((lx, wd), F32)],
        compiler_params=_params("parallel"),
        name="delta_rule",
    )(qc, kc, vc, bac, zc, qx, kx, vx, bax, zx, alog, dtb, gn, e_mean, sel)


def _attn_kernel(q_ref, k_ref, vt_ref, o_ref, qs_ref, s_ref, p_ref, al_ref, m_ref, acc_ref, *, tq, tk, n_k):
    for h in range(ATT_GROUP):
        qs_ref[h * tq:(h + 1) * tq, :] = q_ref[0, :, h * ATT_HD:(h + 1) * ATT_HD]
    m_ref[...] = jnp.full(m_ref.shape, -1e30, F32)
    acc_ref[...] = jnp.zeros(acc_ref.shape, F32)

    def scores(j, slot):
        s_ref[slot] = _dot_nt(k_ref[0, 0, j * tk:(j + 1) * tk, :], qs_ref[...])

    def softmax(slot):
        s = s_ref[slot]
        m_old = m_ref[...]
        m_new = jnp.maximum(m_old, jnp.max(s, axis=0, keepdims=True))
        p_ref[slot] = jnp.exp2(s - m_new).astype(BF16)
        al_ref[slot] = jnp.exp2(m_old - m_new)
        m_ref[...] = m_new

    def values(j, slot):
        acc_ref[...] = acc_ref[...] * al_ref[slot] + _dot(vt_ref[0, 0, :, j * tk:(j + 1) * tk], p_ref[slot])

    for i in range(n_k + 2):
        if i < n_k:
            scores(i, i % 2)
        if 1 <= i <= n_k:
            softmax((i - 1) % 2)
        if i >= 2:
            values(i - 2, i % 2)

    acc = acc_ref[...]
    out = acc[:ATT_HD] / acc[ATT_HD:ATT_HD + 1]
    o_ref[0] = jnp.concatenate([out[:, h * tq:(h + 1) * tq].T for h in range(ATT_GROUP)], axis=1).astype(BF16)


def _attn_call(q, k4, vt):
    nb, lq, _ = q.shape
    s = k4.shape[2]
    tq = _tile(lq, 512)
    tk = 256
    gw = ATT_GROUP * ATT_HD
    cols = ATT_GROUP * tq
    kern = functools.partial(_attn_kernel, tq=tq, tk=tk, n_k=s // tk)
    return pl.pallas_call(
        kern,
        out_shape=jax.ShapeDtypeStruct((nb, lq, Q_W), BF16),
        grid=(nb, ATT_KV, lq // tq),
        in_specs=[
            pl.BlockSpec((1, tq, gw), lambda b, g, i: (b, i, g)),
            pl.BlockSpec((1, 1, s, ATT_HD), lambda b, g, i: (b, g, 0, 0)),
            pl.BlockSpec((1, 1, V_ROWS, s), lambda b, g, i: (b, g, 0, 0)),
        ],
        out_specs=pl.BlockSpec((1, tq, gw), lambda b, g, i: (b, i, g)),
        scratch_shapes=[pltpu.VMEM((cols, ATT_HD), BF16), pltpu.VMEM((2, tk, cols), F32),
                        pltpu.VMEM((2, tk, cols), BF16), pltpu.VMEM((2, 1, cols), F32),
                        pltpu.VMEM((1, cols), F32), pltpu.VMEM((V_ROWS, cols), F32)],
        compiler_params=_params("parallel", "parallel", "arbitrary"),
        name="gqa_attn",
    )(q, k4, vt)


def _merge_kernel(pp_ref, pc_ref, pn_ref, x_ref, mod_ref, yd_ref, ya_ref, gate_ref, pw_ref, ps_ref,
                  wb_ref, wo_ref, o_ref, pad_ref, *, tm, seq):
    d = x_ref.shape[2]
    a0 = POOL_WIDTH
    a1 = POOL_WIDTH + DN_VW
    m = gate_ref[0, :, d:2 * d].astype(F32) * _dot(yd_ref[0], wb_ref[a0:a1, :])
    m = m + gate_ref[0, :, 2 * d:3 * d].astype(F32) * _dot(ya_ref[0], wb_ref[a1:, :])
    _fill_halo(pad_ref, pp_ref, pc_ref, pn_ref, tm)
    pooled = _window_means_minus_self(pad_ref, tm, seq)
    y_pool = (_dot(pooled.astype(BF16), pw_ref[...]) * ps_ref[...]).astype(BF16)
    m = m + gate_ref[0, :, 0:d].astype(F32) * _dot(y_pool, wb_ref[0:a0, :])
    o_ref[0] = x_ref[0] + mod_ref[0, 2] * _dot(m.astype(BF16), wo_ref[...])


def _merge_call(x, mod, sub, pool, y_dn, y_att, gate, pool_bd, pool_s, w_branch, w_out):
    nb, t, d = x.shape
    tm = _tile(t, 512)
    row = lambda w: pl.BlockSpec((1, tm, w), lambda b, i: (b, i, 0))
    kern = functools.partial(_merge_kernel, tm=tm, seq=t)
    return pl.pallas_call(
        kern,
        out_shape=jax.ShapeDtypeStruct(x.shape, F32),
        grid=(nb, t // tm),
        in_specs=_halo_specs(tm, POOL_WIDTH, t) + [
            row(d), _mod_spec(mod, nb, sub, d), row(DN_VW), row(Q_W), row(N_BRANCH * d),
            _resident(pool_bd.shape), _resident(pool_s.shape),
            _resident(w_branch.shape), _resident(w_out.shape)],
        out_specs=row(d),
        scratch_shapes=[pltpu.VMEM((tm + 2 * SUBLANE, POOL_WIDTH), F32)],
        compiler_params=_params("parallel", "arbitrary"),
        name="merge",
    )(pool, pool, pool, x, mod, y_dn, y_att, gate, pool_bd, pool_s, w_branch, w_out)


def _final_norm_kernel(x_ref, g_ref, o_ref):
    x = x_ref[0]
    ms = jnp.mean(x * x, axis=-1, keepdims=True)
    o_ref[0] = x * lax.rsqrt(ms + EPS) * g_ref[...]


def _final_norm_call(x, gain):
    nb, t, d = x.shape
    tm = _tile(t, 512)
    return pl.pallas_call(
        _final_norm_kernel,
        out_shape=jax.ShapeDtypeStruct(x.shape, F32),
        grid=(nb, t // tm),
        in_specs=[pl.BlockSpec((1, tm, d), lambda b, i: (b, i, 0)), _resident((1, d))],
        out_specs=pl.BlockSpec((1, tm, d), lambda b, i: (b, i, 0)),
        compiler_params=_params("parallel", "parallel"),
        name="final_norm",
    )(x, gain.reshape(1, d))


def _block_diag_const(width, block, value):
    idx = np.arange(width)
    return jnp.asarray(np.where((idx[:, None] // block) == (idx[None, :] // block), value, 0.0), BF16)


def _pair_swap_const(width):
    idx = np.arange(width)
    return jnp.asarray((idx[:, None] == (idx[None, :] ^ 1)).astype(np.float32), BF16)


def _rope_tables(n_tokens):
    rows = n_tokens // GRID_W
    row = jnp.broadcast_to(jnp.arange(rows)[:, None], (rows, GRID_W)).reshape(-1)
    col = jnp.broadcast_to(jnp.arange(GRID_W)[None, :], (rows, GRID_W)).reshape(-1)
    n_freq = ATT_HD // 4
    inv = ROPE_BASE ** (-jnp.arange(n_freq, dtype=F32) / n_freq)
    ang = jnp.concatenate([row[:, None].astype(F32) * inv, col[:, None].astype(F32) * inv], axis=-1)
    cos = jnp.repeat(jnp.cos(ang), 2, axis=1)
    sin = jnp.repeat(jnp.sin(ang), 2, axis=1)
    sign = jnp.where(jnp.arange(ATT_HD) % 2 == 0, -1.0, 1.0).astype(F32)
    sin = sin * sign
    reps = LANE // ATT_HD
    return jnp.tile(cos, (1, reps)), jnp.tile(sin, (1, reps))


def _permute_w_in(w_in):
    d = w_in.shape[0]
    o_pool = 0
    o_qkv = o_pool + POOL_WIDTH
    o_z = o_qkv + DN_CONV_CH
    o_ba = o_z + DN_VW
    o_aq = o_ba + 4 * DN_HEADS
    o_gate = o_aq + Q_W + 2 * KV_W
    pad = jnp.zeros((d, BA_PAD - 4 * DN_HEADS), w_in.dtype)
    return jnp.concatenate([w_in[:, o_pool:o_ba], w_in[:, o_aq:], w_in[:, o_ba:o_aq], pad], axis=1).astype(BF16)


def _head_select_const():
    sel = np.zeros((2, LANE, 2 * DN_VW), np.float32)
    for d in range(2):
        for h in range(DN_HEADS):
            sel[d, d * DN_HEADS + h, h * DN_DK:(h + 1) * DN_DK] = 1.0
            sel[d, 2 * DN_HEADS + d * DN_HEADS + h, DN_VW + h * DN_DK:DN_VW + (h + 1) * DN_DK] = 1.0
    return jnp.asarray(sel, BF16)


def _lane_vec(vals, offset):
    out = jnp.zeros((1, LANE), F32)
    return out.at[0, offset:offset + vals.size].set(vals.reshape(-1).astype(F32))


def _kv_layout(k_c, v_c, k_x=None, v_x=None):
    k = k_c if k_x is None else jnp.concatenate([k_c, k_x], axis=1)
    v = v_c if v_x is None else jnp.concatenate([v_c, v_x], axis=1)
    nb, s, _ = k.shape
    k4 = k.reshape(nb, s, ATT_KV, ATT_HD).transpose(0, 2, 1, 3)
    vt = v.reshape(nb, s, ATT_KV, ATT_HD).transpose(0, 2, 3, 1)
    ones = jnp.ones((nb, ATT_KV, 1, s), vt.dtype)
    zeros = jnp.zeros((nb, ATT_KV, V_ROWS - ATT_HD - 1, s), vt.dtype)
    return k4, jnp.concatenate([vt, ones, zeros], axis=2)


def kernel(x, c, ctx, c_ctx, w_ada, b_ada, norm_ffn1, w_ffn1_in, w_ffn1_out, norm_mix, w_in, pool_w, pool_scale, dn_conv, dn_a_log, dn_dt_bias, dn_norm, q_norm, k_norm, w_branch, w_out, norm_ffn2, w_ffn2_in, w_ffn2_out, norm_final):
    nb, seq, d = x.shape
    lc = ctx.shape[1]
    depth = w_ada.shape[0]

    n_rows = -(-(nb + 1) // SUBLANE) * SUBLANE
    cond = jnp.concatenate([c, c_ctx[None], jnp.zeros((n_rows - nb - 1, d), F32)], axis=0)
    mod_all = _ada_call(cond, w_ada, b_ada).reshape(depth, n_rows, N_MOD, 1, d)

    cos_x, sin_x = _rope_tables(seq)
    cos_c = jnp.ones((lc, LANE), F32)
    sin_c = jnp.zeros((lc, LANE), F32)
    e_q = _block_diag_const(Q_W, ATT_HD, 1.0 / ATT_HD)
    e_k = _block_diag_const(KV_W, ATT_HD, 1.0 / ATT_HD)
    p_q = _pair_swap_const(Q_W)
    p_k = _pair_swap_const(KV_W)
    e_sum = _block_diag_const(DN_QK, DN_DK, 1.0)
    e_mean = _block_diag_const(DN_VW, DN_DV, 1.0 / DN_DV)
    head_sel = _head_select_const()

    xc = ctx.reshape(1, nb * lc, d)
    for l in range(depth):
        ctx_out = l < depth - 1
        mod_x = mod_all[l, :nb]
        mod_c = mod_all[l, nb:nb + 1]
        wf1_in = w_ffn1_in[l].astype(BF16)
        wf1_out = w_ffn1_out[l].astype(BF16)
        wf2_in = w_ffn2_in[l].astype(BF16)
        wf2_out = w_ffn2_out[l].astype(BF16)
        w_perm = _permute_w_in(w_in[l])
        w_br = w_branch[l].astype(BF16)
        w_o = w_out[l].astype(BF16)
        qg = jnp.tile(q_norm[l], ATT_HEADS).reshape(1, Q_W)
        kg = jnp.tile(k_norm[l], ATT_KV).reshape(1, KV_W)
        pool_bd = jax.scipy.linalg.block_diag(*[pool_w[l, g] for g in range(len(POOL_WINDOWS))]).astype(BF16)
        pool_s = pool_scale[l].reshape(1, POOL_WIDTH)
        alog = _lane_vec(dn_a_log[l], 2 * DN_HEADS)
        dtb = _lane_vec(dn_dt_bias[l], 2 * DN_HEADS)
        gn = jnp.tile(dn_norm[l], DN_HEADS).reshape(1, DN_VW)

        x = _ffn_call(x, mod_x, 0, norm_ffn1[l], wf1_in, wf1_out)
        xc = _ffn_call(xc, mod_c, 0, norm_ffn1[l], wf1_in, wf1_out)

        proj_consts = (qg, kg, e_q, p_q, e_k, p_k, dn_conv[l], e_sum)
        xc_seq = xc.reshape(nb, lc, d)
        pool_x, qd_x, kd_x, vd_x, z_x, ba_x, aq_x, ak_x, av_x, gate_x = _proj_call(
            x, mod_x, 1, norm_mix[l], w_perm, cos_x, sin_x, *proj_consts)
        pool_c, qd_c, kd_c, vd_c, z_c, ba_c, aq_c, ak_c, av_c, gate_c = _proj_call(
            xc_seq, mod_c, 1, norm_mix[l], w_perm, cos_c, sin_c, *proj_consts)

        yd_c, yd_x = _delta_call((qd_c, kd_c, vd_c, ba_c, z_c), (qd_x, kd_x, vd_x, ba_x, z_x),
                                 alog, dtb, gn, e_mean, head_sel)

        k4_all, vt_all = _kv_layout(ak_c, av_c, ak_x, av_x)
        ya_x = _attn_call(aq_x, k4_all, vt_all)
        x = _merge_call(x, mod_x, 1, pool_x, yd_x, ya_x, gate_x, pool_bd, pool_s, w_br, w_o)

        if ctx_out:
            k4_c, vt_c = _kv_layout(ak_c, av_c)
            ya_c = _attn_call(aq_c, k4_c, vt_c)
            xc = _merge_call(xc_seq, mod_c, 1, pool_c, yd_c, ya_c, gate_c, pool_bd, pool_s, w_br, w_o)
            xc = xc.reshape(1, nb * lc, d)

        x = _ffn_call(x, mod_x, 2, norm_ffn2[l], wf2_in, wf2_out)
        if ctx_out:
            xc = _ffn_call(xc, mod_c, 2, norm_ffn2[l], wf2_in, wf2_out)

    return _final_norm_call(x, norm_final)
```

```python
import functools
import math

import jax
import jax.numpy as jnp
import numpy as np
from jax import lax
from jax.experimental import pallas as pl
from jax.experimental.pallas import tpu as pltpu

F32 = jnp.float32
BF16 = jnp.bfloat16

N_MOD = 9
FFN_RES = 0.5
POOL_WIDTH = 256
POOL_WINDOWS = (2, 4, 8, 16)
POOL_GROUP = 64
DN_HEADS = 4
DN_DK = 64
DN_DV = 64
DN_CONV = 5
DN_CHUNK = 64
DN_QK = DN_HEADS * DN_DK
DN_VW = DN_HEADS * DN_DV
DN_CONV_CH = 2 * DN_QK + DN_VW
ATT_HEADS = 8
ATT_KV = 2
ATT_GROUP = ATT_HEADS // ATT_KV
ATT_HD = 64
ROPE_BASE = 10000.0
GRID_W = 64
N_BRANCH = 3
EPS = 1e-6
LOG2E = 1.4426950408889634

LANE = 128
SUBLANE = 8
VMEM_LIMIT = 56 * 1024 * 1024

BA_PAD = LANE
Q_W = ATT_HEADS * ATT_HD
KV_W = ATT_KV * ATT_HD
V_ROWS = ATT_HD + 16
C_POOL = 0
C_QKV = C_POOL + POOL_WIDTH
C_Z = C_QKV + DN_CONV_CH
C_AQ = C_Z + DN_VW
C_AK = C_AQ + Q_W
C_AV = C_AK + KV_W
C_GATE = C_AV + KV_W


def _params(*sem):
    return pltpu.CompilerParams(dimension_semantics=sem, vmem_limit_bytes=VMEM_LIMIT)


def _tile(n, target):
    t = min(target, n)
    assert n % t == 0 and t % SUBLANE == 0, (n, t)
    return t


def _resident(shape):
    nd = len(shape)
    return pl.BlockSpec(shape, lambda *_: (0,) * nd, pipeline_mode=pl.Buffered(1))


def _silu(x):
    return x * jax.nn.sigmoid(x)


def _dot(a, b):
    return jnp.dot(a, b, preferred_element_type=F32)


def _dot_nt(a, b):
    return lax.dot_general(a, b, (((1,), (1,)), ((), ())), preferred_element_type=F32)


def _dot_tn(a, b):
    return lax.dot_general(a, b, (((0,), (0,)), ((), ())), preferred_element_type=F32)


def _split3(x):
    hi = x.astype(BF16)
    r1 = x - hi.astype(F32)
    mid = r1.astype(BF16)
    lo = (r1 - mid.astype(F32)).astype(BF16)
    return hi, mid, lo


def _dot_exact_lhs(a_bf16, x_f32):
    hi, mid, lo = _split3(x_f32)
    return _dot(a_bf16, hi) + _dot(a_bf16, mid) + _dot(a_bf16, lo)


def _dot_exact_rhs(x_f32, b_bf16):
    hi, mid, lo = _split3(x_f32)
    return _dot(hi, b_bf16) + _dot(mid, b_bf16) + _dot(lo, b_bf16)


def _ada_kernel(s_ref, w_ref, b_ref, o_ref):
    s = _silu(s_ref[...])
    w = w_ref[0]
    s_hi, s_mid, s_lo = _split3(s)
    w_hi, w_mid, w_lo = _split3(w)
    acc = _dot(s_hi, w_hi)
    acc = acc + _dot(s_hi, w_mid) + _dot(s_mid, w_hi)
    acc = acc + _dot(s_hi, w_lo) + _dot(s_mid, w_mid) + _dot(s_lo, w_hi)
    o_ref[0] = acc + b_ref[0]


def _ada_call(cond, w_ada, b_ada):
    depth, d, n = w_ada.shape
    rows = cond.shape[0]
    tn = 1024
    return pl.pallas_call(
        _ada_kernel,
        out_shape=jax.ShapeDtypeStruct((depth, rows, n), F32),
        grid=(depth, n // tn),
        in_specs=[
            pl.BlockSpec((rows, d), lambda l, j: (0, 0)),
            pl.BlockSpec((1, d, tn), lambda l, j: (l, 0, j)),
            pl.BlockSpec((1, 1, tn), lambda l, j: (l, 0, j)),
        ],
        out_specs=pl.BlockSpec((1, rows, tn), lambda l, j: (l, 0, j)),
        compiler_params=_params("parallel", "parallel"),
        name="ada_mod",
    )(cond, w_ada, b_ada.reshape(depth, 1, n))


def _norm_mod(x, gain, shift, scale):
    ms = jnp.mean(x * x, axis=-1, keepdims=True)
    h = x * lax.rsqrt(ms + EPS) * gain
    return h * (1.0 + scale) + shift


def _ffn_kernel(x_ref, mod_ref, g_ref, wi_ref, wo_ref, o_ref, acc_ref, *, d_ff, tf):
    x = x_ref[0]
    h = _norm_mod(x, g_ref[...], mod_ref[0, 0], mod_ref[0, 1]).astype(BF16)
    for j in range(d_ff // tf):
        g = _dot(h, wi_ref[:, j * tf:(j + 1) * tf])
        u = _dot(h, wi_ref[:, d_ff + j * tf:d_ff + (j + 1) * tf])
        a = (_silu(g) * u).astype(BF16)
        part = _dot(a, wo_ref[j * tf:(j + 1) * tf, :])
        if j == 0:
            acc_ref[...] = part
        else:
            acc_ref[...] += part
    o_ref[0] = x + (FFN_RES * mod_ref[0, 2]) * acc_ref[...]


def _ffn_call(x, mod, sub, gain, w_in, w_out):
    nb, t, d = x.shape
    d_ff = w_out.shape[0]
    tm = _tile(t, 512)
    tf = 256
    kern = functools.partial(_ffn_kernel, d_ff=d_ff, tf=tf)
    return pl.pallas_call(
        kern,
        out_shape=jax.ShapeDtypeStruct(x.shape, F32),
        grid=(nb, t // tm),
        in_specs=[
            pl.BlockSpec((1, tm, d), lambda b, i: (b, i, 0)),
            pl.BlockSpec((1, 3, 1, d), lambda b, i: (b, sub, 0, 0)),
            _resident((1, d)),
            _resident((d, 2 * d_ff)),
            _resident((d_ff, d)),
        ],
        out_specs=pl.BlockSpec((1, tm, d), lambda b, i: (b, i, 0)),
        scratch_shapes=[pltpu.VMEM((tm, d), F32)],
        compiler_params=_params("parallel", "parallel"),
        name="ffn",
    )(x, mod, gain.reshape(1, d), w_in, w_out)


def _head_norm_rope(t_raw, gain, e_mat, p_mat, cos, sin, out_scale):
    ms = _dot((t_raw * t_raw).astype(BF16), e_mat)
    r = lax.rsqrt(ms + EPS) * out_scale
    t = t_raw * gain
    t_sw = _dot(t.astype(BF16), p_mat)
    w = t_raw.shape[1]
    outs = []
    for g in range(w // LANE):
        sl = slice(g * LANE, (g + 1) * LANE)
        outs.append(r[:, sl] * (t[:, sl] * cos + t_sw[:, sl] * sin))
    return jnp.concatenate(outs, axis=1) if len(outs) > 1 else outs[0]


def _proj_kernel(xp_ref, x_ref, xn_ref, mod_ref, g_ref, w_ref, cos_ref, sin_ref, qg_ref, kg_ref,
                 eq_ref, pq_ref, ek_ref, pk_ref, cw_ref, es_ref,
                 pool_ref, dq_ref, dk_ref, dv_ref, z_ref, ba_ref, q_ref, k_ref, v_ref, gate_ref,
                 pad_ref, *, n_gate, tm):
    gain = g_ref[...]
    shift = mod_ref[0, 0]
    scale = mod_ref[0, 1]
    h = _norm_mod(x_ref[0], gain, shift, scale).astype(BF16)

    def cols(lo, width):
        return _dot(h, w_ref[:, lo:lo + width])

    pool_ref[0] = cols(C_POOL, POOL_WIDTH)

    i = pl.program_id(1)
    n = pl.num_programs(1)
    halo = jnp.concatenate([xp_ref[0], xn_ref[0]], axis=0)
    h_halo = _norm_mod(halo, gain, shift, scale).astype(BF16)
    w_qkv = w_ref[:, C_QKV:C_QKV + DN_CONV_CH]
    qkv_halo = _dot(h_halo, w_qkv)
    pad_ref[0:SUBLANE, :] = jnp.where(i > 0, qkv_halo[0:SUBLANE], 0.0)
    pad_ref[SUBLANE + tm:2 * SUBLANE + tm, :] = jnp.where(i < n - 1, qkv_halo[SUBLANE:2 * SUBLANE], 0.0)
    for j in range(DN_CONV_CH // 256):
        pad_ref[SUBLANE:SUBLANE + tm, j * 256:(j + 1) * 256] = cols(C_QKV + j * 256, 256)
    z_ref[0] = cols(C_Z, DN_VW).astype(BF16)
    q_raw = cols(C_AQ, Q_W)
    k_raw = cols(C_AK, KV_W)
    v_ref[0] = cols(C_AV, KV_W).astype(BF16)
    tg = 512
    for j in range(n_gate // tg):
        gate_ref[0, :, j * tg:(j + 1) * tg] = jax.nn.sigmoid(cols(C_GATE + j * tg, tg)).astype(BF16)
    ba_ref[0] = cols(C_GATE + n_gate, BA_PAD)

    cos = cos_ref[...]
    sin = sin_ref[...]
    q = _head_norm_rope(q_raw, qg_ref[...], eq_ref[...], pq_ref[...], cos, sin, (ATT_HD ** -0.5) * LOG2E)
    q_ref[0] = q.astype(BF16)
    k = _head_norm_rope(k_raw, kg_ref[...], ek_ref[...], pk_ref[...], cos, sin, 1.0)
    k_ref[0] = k.astype(BF16)

    half = DN_CONV // 2
    n_pad = tm + 2 * SUBLANE
    for part, o_ref in enumerate((dq_ref, dk_ref, dv_ref)):
        cs = slice(part * DN_QK, (part + 1) * DN_QK)
        xp = pad_ref[:, cs]
        acc = None
        for j in range(DN_CONV):
            shifted = xp if j == half else pltpu.roll(xp, (half - j) % n_pad, 0)
            term = shifted[SUBLANE:SUBLANE + tm] * cw_ref[j:j + 1, cs]
            acc = term if acc is None else acc + term
        y = _silu(acc)
        if part < 2:
            y = y * lax.rsqrt(_dot((y * y).astype(BF16), es_ref[...]) + EPS)
            if part == 0:
                y = y * (DN_DK ** -0.5)
        o_ref[0] = y.astype(BF16)


def _mod_spec(mod, nb, sub, d):
    if mod.shape[0] == nb:
        return pl.BlockSpec((1, 3, 1, d), lambda b, i: (b, sub, 0, 0))
    return pl.BlockSpec((1, 3, 1, d), lambda b, i: (0, sub, 0, 0))


def _proj_call(x, mod, sub, gain, w_perm, cos, sin, qg, kg, eq, pq, ek, pk, conv_w, e_sum):
    nb, t, d = x.shape
    n_gate = N_BRANCH * d
    ncol = w_perm.shape[1]
    tm = _tile(t, 512)
    nt = t // tm
    row = lambda w: pl.BlockSpec((1, tm, w), lambda b, i: (b, i, 0))
    widths = [(POOL_WIDTH, F32), (DN_QK, BF16), (DN_QK, BF16), (DN_VW, BF16), (DN_VW, BF16), (BA_PAD, F32),
              (Q_W, BF16), (KV_W, BF16), (KV_W, BF16), (n_gate, BF16)]
    kern = functools.partial(_proj_kernel, n_gate=n_gate, tm=tm)
    return pl.pallas_call(
        kern,
        out_shape=[jax.ShapeDtypeStruct((nb, t, wd), dt) for wd, dt in widths],
        grid=(nb, nt),
        in_specs=_halo_specs(tm, d, t) + [
            _mod_spec(mod, nb, sub, d),
            _resident((1, d)),
            _resident((d, ncol)),
            pl.BlockSpec((tm, LANE), lambda b, i: (i, 0)),
            pl.BlockSpec((tm, LANE), lambda b, i: (i, 0)),
            _resident((1, Q_W)),
            _resident((1, KV_W)),
            _resident((Q_W, Q_W)),
            _resident((Q_W, Q_W)),
            _resident((KV_W, KV_W)),
            _resident((KV_W, KV_W)),
            _resident(conv_w.shape),
            _resident(e_sum.shape),
        ],
        out_specs=[row(wd) for wd, _ in widths],
        scratch_shapes=[pltpu.VMEM((tm + 2 * SUBLANE, DN_CONV_CH), F32)],
        compiler_params=_params("parallel", "arbitrary"),
        name="mix_proj",
    )(x, x, x, mod, gain.reshape(1, d), w_perm, cos, sin, qg, kg, eq, pq, ek, pk, conv_w, e_sum)


def _fill_halo(pad_ref, prev_ref, cur_ref, next_ref, tt):
    i = pl.program_id(1)
    n = pl.num_programs(1)
    prev = prev_ref[0]
    nxt = next_ref[0]
    pad_ref[0:SUBLANE, :] = jnp.where(i > 0, prev, jnp.zeros_like(prev))
    pad_ref[SUBLANE:SUBLANE + tt, :] = cur_ref[0]
    pad_ref[SUBLANE + tt:2 * SUBLANE + tt, :] = jnp.where(i < n - 1, nxt, jnp.zeros_like(nxt))


def _halo_specs(tt, width, n_rows):
    per = tt // SUBLANE
    last = n_rows // SUBLANE - 1
    return [
        pl.BlockSpec((1, SUBLANE, width), lambda b, i: (b, jnp.maximum(i * per - 1, 0), 0)),
        pl.BlockSpec((1, tt, width), lambda b, i: (b, i, 0)),
        pl.BlockSpec((1, SUBLANE, width), lambda b, i: (b, jnp.minimum((i + 1) * per, last), 0)),
    ]


def _window_means_minus_self(pad_ref, tt, seq):
    assert POOL_WINDOWS == (2, 4, 8, 16) and POOL_GROUP * 2 == LANE
    i = pl.program_id(1)
    t = i * tt + lax.broadcasted_iota(jnp.int32, (tt, 1), 0)
    lane = lax.broadcasted_iota(jnp.int32, (1, LANE), 1)
    n = tt + 2 * SUBLANE
    body = slice(SUBLANE, SUBLANE + tt)

    def count(w):
        return (jnp.minimum(t + w - w // 2, seq) - jnp.maximum(t - w // 2, 0)).astype(F32)

    def ahead(a, k):
        return pltpu.roll(a, n - k, 0)

    def behind(a, k):
        return pltpu.roll(a, k, 0)

    tiles = []
    for lt in range(POOL_WIDTH // LANE):
        x = pad_ref[:, lt * LANE:(lt + 1) * LANE]
        r2 = x + ahead(x, 1)
        r4 = r2 + ahead(r2, 2)
        if lt == 0:
            narrow = behind(r2, 1)[body] / count(2)
            wide = behind(r4, 2)[body] / count(4)
        else:
            r8 = r4 + ahead(r4, 4)
            narrow = behind(r8, 4)[body] / count(8)
            wide = (r8[0:tt] + r8[body]) / count(16)
        tiles.append(jnp.where(lane >= POOL_GROUP, wide, narrow) - x[body])
    return jnp.concatenate(tiles, axis=1)


def _delta_kernel(qc_ref, kc_ref, vc_ref, bac_ref, zc_ref, qx_ref, kx_ref, vx_ref, bax_ref, zx_ref,
                  alog_ref, dtb_ref, gn_ref, e_ref, sel_ref, yc_ref, yx_ref,
                  sf_ref, sb_ref, ocf_ref, ocb_ref, oxf_ref, oxb_ref, *, n_ctx, n_lat, group):
    c = DN_CHUNK
    w = DN_HEADS * c
    row = lax.broadcasted_iota(jnp.int32, (w, w), 0)
    col = lax.broadcasted_iota(jnp.int32, (w, w), 1)
    rc_xor = row ^ col
    same = (rc_xor >> 6) == 0
    rp = row & (c - 1)
    cp = col & (c - 1)
    eye = jnp.where(row == col, 1.0, 0.0).astype(F32)
    m_incl = (same & (rp >= cp), same & (rp <= cp))
    m_strict = (same & (rp > cp), same & (rp < cp))
    r64 = lax.broadcasted_iota(jnp.int32, (c, c), 0)
    c64 = lax.broadcasted_iota(jnp.int32, (c, c), 1)
    cum_mat = (jnp.where(r64 >= c64, 1.0, 0.0).astype(BF16), jnp.where(r64 <= c64, 1.0, 0.0).astype(BF16))
    neg_a = -jnp.exp(alog_ref[...])
    dtb = dtb_ref[...]

    sf_ref[...] = jnp.zeros_like(sf_ref)
    sb_ref[...] = jnp.zeros_like(sb_ref)

    def spread(a):
        return jnp.where(same, jnp.concatenate([a, a, a, a], axis=0), 0.0)

    def gather_heads(o):
        return o[0:c] + o[c:2 * c] + o[2 * c:3 * c] + o[3 * c:4 * c]

    def prepare(refs, chunks):
        q_ref, k_ref, v_ref, ba_ref = refs
        st = []
        for rows, d in chunks:
            q = q_ref[0, rows, :].astype(F32)
            k = k_ref[0, rows, :].astype(F32)
            v = v_ref[0, rows, :].astype(F32)
            ba = ba_ref[0, rows, :]
            g_all = neg_a * jax.nn.softplus(ba + dtb)
            cum = _dot_exact_lhs(cum_mat[d], g_all)
            st.append(dict(d=d, q=q, k=k, v=v, ba=ba, cum=cum))
        for x in st:
            sel = sel_ref[x["d"]]
            x["beta"] = _dot(jax.nn.sigmoid(x["ba"]).astype(BF16), sel[:, :w])
            x["g_cum"] = _dot_exact_rhs(x["cum"], sel[:, w:])
        for x in st:
            x["kb"] = x["k"] * x["beta"]
            k_bd = spread(x["k"]).astype(BF16)
            x["prod"] = _dot_nt(jnp.concatenate([spread(x["kb"]), spread(x["q"])], axis=0).astype(BF16), k_bd)
        for x in st:
            d = x["d"]
            g_col = spread(x["g_cum"])
            decay = jnp.where(m_incl[d], jnp.exp(g_col - g_col.T), 0.0)
            a = jnp.where(m_strict[d], x["prod"][:w] * decay, 0.0)
            x["qkd"] = (x["prod"][w:] * decay).astype(BF16)
            x["a"] = a
            a4 = jnp.where((rc_xor >> 2) == 0, a, 0.0)
            x["t0"] = eye - a4
            x["a4"] = a4.astype(BF16)
        for x in st:
            x["a4sq"] = _dot(x["a4"], x["a4"]).astype(BF16)
        for x in st:
            x["t"] = x["t0"] + _dot(x["t0"].astype(BF16), x["a4sq"])
        for lg in range(2, 6):
            for x in st:
                b_ring = jnp.where((rc_xor >> lg) == 1, x["a"], 0.0).astype(BF16)
                x["t_b"] = x["t"].astype(BF16)
                x["tb"] = _dot(x["t_b"], b_ring).astype(BF16)
            for x in st:
                x["t"] = x["t"] - _dot(x["tb"], x["t_b"])
        out = []
        for x in st:
            last = c - 1 if x["d"] == 0 else 0
            g_cum = x["g_cum"]
            g_tot = g_cum[last:last + 1, :]
            e_g = jnp.exp(g_cum)
            rhs = jnp.concatenate([spread(x["v"] * x["beta"]), spread(x["kb"] * e_g)], axis=1).astype(BF16)
            uw = _dot(x["t"].astype(BF16), rhs)
            out.append(dict(u=uw[:, :w],
                            lhs=jnp.concatenate([uw[:, w:], spread(x["q"] * e_g)], axis=0).astype(BF16),
                            qkd=x["qkd"], kd=spread(x["k"] * jnp.exp(g_tot - g_cum)).astype(BF16),
                            e_tot=jnp.exp(g_tot)))
        return out

    def advance(pre, s_ref):
        s_old = s_ref[...]
        ws_qs = _dot(pre["lhs"], s_old.astype(BF16))
        v_new = (pre["u"] - ws_qs[:w]).astype(BF16)
        o = ws_qs[w:] + _dot(pre["qkd"], v_new)
        s_ref[...] = s_old * pre["e_tot"] + _dot_tn(pre["kd"], v_new)
        return gather_heads(o)

    def run_block(refs, of_ref, ob_ref, rows_f, rows_b):
        chunks = []
        for g in range(len(rows_f)):
            chunks += [(rows_f[g], 0), (rows_b[g], 1)]
        pre = prepare(refs, chunks)
        for g in range(len(rows_f)):
            of_ref[rows_f[g], :] = advance(pre[2 * g], sf_ref)
            ob_ref[rows_b[g], :] = advance(pre[2 * g + 1], sb_ref)

    ctx_refs = (qc_ref, kc_ref, vc_ref, bac_ref)
    lat_refs = (qx_ref, kx_ref, vx_ref, bax_ref)
    for j in range(n_ctx // group):
        run_block(ctx_refs, ocf_ref, ocb_ref,
                  [pl.ds((j * group + g) * c, c) for g in range(group)],
                  [pl.ds((n_ctx - 1 - j * group - g) * c, c) for g in range(group)])

    def lat_body(j, carry):
        rows_f = [pl.ds(pl.multiple_of((j * group + g) * c, c), c) for g in range(group)]
        rows_b = [pl.ds(pl.multiple_of((n_lat - 1 - j * group - g) * c, c), c) for g in range(group)]
        run_block(lat_refs, oxf_ref, oxb_ref, rows_f, rows_b)
        return carry

    lax.fori_loop(0, n_lat // group, lat_body, 0)

    def finish(of_ref, ob_ref, z_ref, y_ref, n_rows):
        tt = min(512, n_rows)
        for j in range(n_rows // tt):
            rs = slice(j * tt, (j + 1) * tt)
            o = of_ref[rs, :] + ob_ref[rs, :]
            ms = _dot((o * o).astype(BF16), e_ref[...])
            y = o * lax.rsqrt(ms + EPS) * gn_ref[...]
            y_ref[0, rs, :] = (y * _silu(z_ref[0, rs, :].astype(F32))).astype(BF16)

    finish(ocf_ref, ocb_ref, zc_ref, yc_ref, n_ctx * c)
    finish(oxf_ref, oxb_ref, zx_ref, yx_ref, n_lat * c)


def _delta_call(ctx_parts, lat_parts, alog, dtb, gn, e_mean, sel):
    qc, kc, vc, bac, zc = ctx_parts
    qx, kx, vx, bax, zx = lat_parts
    nb, lc, wd = qc.shape
    lx = qx.shape[1]
    group = 2
    assert lc % (group * DN_CHUNK) == 0 and lx % (group * DN_CHUNK) == 0, (lc, lx)
    assert lc % _tile(lc, 512) == 0 and lx % _tile(lx, 512) == 0
    kern = functools.partial(_delta_kernel, n_ctx=lc // DN_CHUNK, n_lat=lx // DN_CHUNK, group=group)
    seq = lambda n, width: pl.BlockSpec((1, n, width), lambda b: (b, 0, 0))
    in_specs = [seq(lc, wd), seq(lc, wd), seq(lc, wd), seq(lc, BA_PAD), seq(lc, wd),
                seq(lx, wd), seq(lx, wd), seq(lx, wd), seq(lx, BA_PAD), seq(lx, wd),
                _resident((1, LANE)), _resident((1, LANE)), _resident((1, wd)), _resident((wd, wd)),
                _resident(sel.shape)]
    return pl.pallas_call(
        kern,
        out_shape=[jax.ShapeDtypeStruct((nb, lc, wd), BF16), jax.ShapeDtypeStruct((nb, lx, wd), BF16)],
        grid=(nb,),
        in_specs=in_specs,
        out_specs=[seq(lc, wd), seq(lx, wd)],
        scratch_shapes=[pltpu.VMEM((wd, wd), F32), pltpu.VMEM((wd, wd), F32),
                        pltpu.VMEM((lc, wd), F32), pltpu.VMEM((lc, wd), F32),
                        pltpu.VMEM((lx, wd), F32), pltpu.VMEM((lx, wd), F32)],
        compiler_params=_params("parallel"),
        name="delta_rule",
    )(qc, kc, vc, bac, zc, qx, kx, vx, bax, zx, alog, dtb, gn, e_mean, sel)


P_CLAMP = 100.0
P_OVERFLOW = 2.0 ** 99


def _attn_kernel(q_ref, k_ref, vt_ref, o_ref, qs_ref, s_ref, p_ref, al_ref, m_ref, acc_ref, *, tq, tk, n_k):
    for h in range(ATT_GROUP):
        qs_ref[h * tq:(h + 1) * tq, :] = q_ref[0, :, h * ATT_HD:(h + 1) * ATT_HD]

    def keys(j):
        return k_ref[0, 0, j * tk:(j + 1) * tk, :]

    def vals(j):
        return vt_ref[0, 0, :, j * tk:(j + 1) * tk]

    s0 = _dot_nt(keys(0), qs_ref[...])
    m0 = jnp.max(s0, axis=0, keepdims=True)
    m_ref[...] = m0
    acc_ref[...] = _dot(vals(0), jnp.exp2(s0 - m0).astype(BF16))

    def probs(j, slot):
        s = _dot_nt(keys(j), qs_ref[...])
        p_ref[slot] = jnp.exp2(jnp.minimum(s - m_ref[...], P_CLAMP)).astype(BF16)

    def accumulate(j, slot):
        acc_ref[...] += _dot(vals(j), p_ref[slot])

    for i in range(1, n_k + 1):
        if i < n_k:
            probs(i, i % 2)
        if i >= 2:
            accumulate(i - 1, (i - 1) % 2)

    @pl.when(jnp.max(acc_ref[ATT_HD:ATT_HD + 1, :]) >= P_OVERFLOW)
    def _():
        m_ref[...] = jnp.full(m_ref.shape, -1e30, F32)
        acc_ref[...] = jnp.zeros(acc_ref.shape, F32)

        def scores(j, slot):
            s_ref[slot] = _dot_nt(keys(j), qs_ref[...])

        def softmax(slot):
            s = s_ref[slot]
            m_old = m_ref[...]
            m_new = jnp.maximum(m_old, jnp.max(s, axis=0, keepdims=True))
            p_ref[slot] = jnp.exp2(s - m_new).astype(BF16)
            al_ref[slot] = jnp.exp2(m_old - m_new)
            m_ref[...] = m_new

        def values(j, slot):
            acc_ref[...] = acc_ref[...] * al_ref[slot] + _dot(vals(j), p_ref[slot])

        for i in range(n_k + 2):
            if i < n_k:
                scores(i, i % 2)
            if 1 <= i <= n_k:
                softmax((i - 1) % 2)
            if i >= 2:
                values(i - 2, i % 2)

    acc = acc_ref[...]
    out = acc[:ATT_HD] / acc[ATT_HD:ATT_HD + 1]
    o_ref[0] = jnp.concatenate([out[:, h * tq:(h + 1) * tq].T for h in range(ATT_GROUP)], axis=1).astype(BF16)


def _attn_call(q, k4, vt):
    nb, lq, _ = q.shape
    s = k4.shape[2]
    tq = _tile(lq, 512)
    tk = 256
    assert s % tk == 0, s
    gw = ATT_GROUP * ATT_HD
    cols = ATT_GROUP * tq
    kern = functools.partial(_attn_kernel, tq=tq, tk=tk, n_k=s // tk)
    return pl.pallas_call(
        kern,
        out_shape=jax.ShapeDtypeStruct((nb, lq, Q_W), BF16),
        grid=(nb, ATT_KV, lq // tq),
        in_specs=[
            pl.BlockSpec((1, tq, gw), lambda b, g, i: (b, i, g)),
            pl.BlockSpec((1, 1, s, ATT_HD), lambda b, g, i: (b, g, 0, 0)),
            pl.BlockSpec((1, 1, V_ROWS, s), lambda b, g, i: (b, g, 0, 0)),
        ],
        out_specs=pl.BlockSpec((1, tq, gw), lambda b, g, i: (b, i, g)),
        scratch_shapes=[pltpu.VMEM((cols, ATT_HD), BF16), pltpu.VMEM((2, tk, cols), F32),
                        pltpu.VMEM((2, tk, cols), BF16), pltpu.VMEM((2, 1, cols), F32),
                        pltpu.VMEM((1, cols), F32), pltpu.VMEM((V_ROWS, cols), F32)],
        compiler_params=_params("parallel", "parallel", "arbitrary"),
        name="gqa_attn",
    )(q, k4, vt)


def _merge_kernel(pp_ref, pc_ref, pn_ref, x_ref, mod_ref, yd_ref, ya_ref, gate_ref, pw_ref, ps_ref,
                  wb_ref, wo_ref, o_ref, pad_ref, *, tm, seq):
    d = x_ref.shape[2]
    a0 = POOL_WIDTH
    a1 = POOL_WIDTH + DN_VW
    m = gate_ref[0, :, d:2 * d].astype(F32) * _dot(yd_ref[0], wb_ref[a0:a1, :])
    m = m + gate_ref[0, :, 2 * d:3 * d].astype(F32) * _dot(ya_ref[0], wb_ref[a1:, :])
    _fill_halo(pad_ref, pp_ref, pc_ref, pn_ref, tm)
    pooled = _window_means_minus_self(pad_ref, tm, seq)
    y_pool = (_dot(pooled.astype(BF16), pw_ref[...]) * ps_ref[...]).astype(BF16)
    m = m + gate_ref[0, :, 0:d].astype(F32) * _dot(y_pool, wb_ref[0:a0, :])
    o_ref[0] = x_ref[0] + mod_ref[0, 2] * _dot(m.astype(BF16), wo_ref[...])


def _merge_call(x, mod, sub, pool, y_dn, y_att, gate, pool_bd, pool_s, w_branch, w_out):
    nb, t, d = x.shape
    tm = _tile(t, 512)
    row = lambda w: pl.BlockSpec((1, tm, w), lambda b, i: (b, i, 0))
    kern = functools.partial(_merge_kernel, tm=tm, seq=t)
    return pl.pallas_call(
        kern,
        out_shape=jax.ShapeDtypeStruct(x.shape, F32),
        grid=(nb, t // tm),
        in_specs=_halo_specs(tm, POOL_WIDTH, t) + [
            row(d), _mod_spec(mod, nb, sub, d), row(DN_VW), row(Q_W), row(N_BRANCH * d),
            _resident(pool_bd.shape), _resident(pool_s.shape),
            _resident(w_branch.shape), _resident(w_out.shape)],
        out_specs=row(d),
        scratch_shapes=[pltpu.VMEM((tm + 2 * SUBLANE, POOL_WIDTH), F32)],
        compiler_params=_params("parallel", "arbitrary"),
        name="merge",
    )(pool, pool, pool, x, mod, y_dn, y_att, gate, pool_bd, pool_s, w_branch, w_out)


def _final_norm_kernel(x_ref, g_ref, o_ref):
    x = x_ref[0]
    ms = jnp.mean(x * x, axis=-1, keepdims=True)
    o_ref[0] = x * lax.rsqrt(ms + EPS) * g_ref[...]


def _final_norm_call(x, gain):
    nb, t, d = x.shape
    tm = _tile(t, 512)
    return pl.pallas_call(
        _final_norm_kernel,
        out_shape=jax.ShapeDtypeStruct(x.shape, F32),
        grid=(nb, t // tm),
        in_specs=[pl.BlockSpec((1, tm, d), lambda b, i: (b, i, 0)), _resident((1, d))],
        out_specs=pl.BlockSpec((1, tm, d), lambda b, i: (b, i, 0)),
        compiler_params=_params("parallel", "parallel"),
        name="final_norm",
    )(x, gain.reshape(1, d))


def _block_diag_const(width, block, value):
    idx = np.arange(width)
    return jnp.asarray(np.where((idx[:, None] // block) == (idx[None, :] // block), value, 0.0), BF16)


def _pair_swap_const(width):
    idx = np.arange(width)
    return jnp.asarray((idx[:, None] == (idx[None, :] ^ 1)).astype(np.float32), BF16)


def _rope_tables(n_tokens):
    rows = n_tokens // GRID_W
    row = jnp.broadcast_to(jnp.arange(rows)[:, None], (rows, GRID_W)).reshape(-1)
    col = jnp.broadcast_to(jnp.arange(GRID_W)[None, :], (rows, GRID_W)).reshape(-1)
    n_freq = ATT_HD // 4
    inv = ROPE_BASE ** (-jnp.arange(n_freq, dtype=F32) / n_freq)
    ang = jnp.concatenate([row[:, None].astype(F32) * inv, col[:, None].astype(F32) * inv], axis=-1)
    cos = jnp.repeat(jnp.cos(ang), 2, axis=1)
    sin = jnp.repeat(jnp.sin(ang), 2, axis=1)
    sign = jnp.where(jnp.arange(ATT_HD) % 2 == 0, -1.0, 1.0).astype(F32)
    sin = sin * sign
    reps = LANE // ATT_HD
    return jnp.tile(cos, (1, reps)), jnp.tile(sin, (1, reps))


def _permute_w_in(w_in):
    d = w_in.shape[0]
    o_pool = 0
    o_qkv = o_pool + POOL_WIDTH
    o_z = o_qkv + DN_CONV_CH
    o_ba = o_z + DN_VW
    o_aq = o_ba + 4 * DN_HEADS
    o_gate = o_aq + Q_W + 2 * KV_W
    pad = jnp.zeros((d, BA_PAD - 4 * DN_HEADS), w_in.dtype)
    return jnp.concatenate([w_in[:, o_pool:o_ba], w_in[:, o_aq:], w_in[:, o_ba:o_aq], pad], axis=1).astype(BF16)


def _head_select_const():
    sel = np.zeros((2, LANE, 2 * DN_VW), np.float32)
    for d in range(2):
        for h in range(DN_HEADS):
            sel[d, d * DN_HEADS + h, h * DN_DK:(h + 1) * DN_DK] = 1.0
            sel[d, 2 * DN_HEADS + d * DN_HEADS + h, DN_VW + h * DN_DK:DN_VW + (h + 1) * DN_DK] = 1.0
    return jnp.asarray(sel, BF16)


def _lane_vec(vals, offset):
    out = jnp.zeros((1, LANE), F32)
    return out.at[0, offset:offset + vals.size].set(vals.reshape(-1).astype(F32))


def _kv_layout(k_c, v_c, k_x=None, v_x=None):
    k = k_c if k_x is None else jnp.concatenate([k_c, k_x], axis=1)
    v = v_c if v_x is None else jnp.concatenate([v_c, v_x], axis=1)
    nb, s, _ = k.shape
    k4 = k.reshape(nb, s, ATT_KV, ATT_HD).transpose(0, 2, 1, 3)
    vt = v.reshape(nb, s, ATT_KV, ATT_HD).transpose(0, 2, 3, 1)
    ones = jnp.ones((nb, ATT_KV, 1, s), vt.dtype)
    zeros = jnp.zeros((nb, ATT_KV, V_ROWS - ATT_HD - 1, s), vt.dtype)
    return k4, jnp.concatenate([vt, ones, zeros], axis=2)


def kernel(x, c, ctx, c_ctx, w_ada, b_ada, norm_ffn1, w_ffn1_in, w_ffn1_out, norm_mix, w_in, pool_w, pool_scale, dn_conv, dn_a_log, dn_dt_bias, dn_norm, q_norm, k_norm, w_branch, w_out, norm_ffn2, w_ffn2_in, w_ffn2_out, norm_final):
    nb, seq, d = x.shape
    lc = ctx.shape[1]
    depth = w_ada.shape[0]

    n_rows = -(-(nb + 1) // SUBLANE) * SUBLANE
    cond = jnp.concatenate([c, c_ctx[None], jnp.zeros((n_rows - nb - 1, d), F32)], axis=0)
    mod_all = _ada_call(cond, w_ada, b_ada).reshape(depth, n_rows, N_MOD, 1, d)

    cos_x, sin_x = _rope_tables(seq)
    cos_c = jnp.ones((lc, LANE), F32)
    sin_c = jnp.zeros((lc, LANE), F32)
    e_q = _block_diag_const(Q_W, ATT_HD, 1.0 / ATT_HD)
    e_k = _block_diag_const(KV_W, ATT_HD, 1.0 / ATT_HD)
    p_q = _pair_swap_const(Q_W)
    p_k = _pair_swap_const(KV_W)
    e_sum = _block_diag_const(DN_QK, DN_DK, 1.0)
    e_mean = _block_diag_const(DN_VW, DN_DV, 1.0 / DN_DV)
    head_sel = _head_select_const()

    xc = ctx.reshape(1, nb * lc, d)
    for l in range(depth):
        ctx_out = l < depth - 1
        mod_x = mod_all[l, :nb]
        mod_c = mod_all[l, nb:nb + 1]
        wf1_in = w_ffn1_in[l].astype(BF16)
        wf1_out = w_ffn1_out[l].astype(BF16)
        wf2_in = w_ffn2_in[l].astype(BF16)
        wf2_out = w_ffn2_out[l].astype(BF16)
        w_perm = _permute_w_in(w_in[l])
        w_br = w_branch[l].astype(BF16)
        w_o = w_out[l].astype(BF16)
        qg = jnp.tile(q_norm[l], ATT_HEADS).reshape(1, Q_W)
        kg = jnp.tile(k_norm[l], ATT_KV).reshape(1, KV_W)
        pool_bd = jax.scipy.linalg.block_diag(*[pool_w[l, g] for g in range(len(POOL_WINDOWS))]).astype(BF16)
        pool_s = pool_scale[l].reshape(1, POOL_WIDTH)
        alog = _lane_vec(dn_a_log[l], 2 * DN_HEADS)
        dtb = _lane_vec(dn_dt_bias[l], 2 * DN_HEADS)
        gn = jnp.tile(dn_norm[l], DN_HEADS).reshape(1, DN_VW)

        x = _ffn_call(x, mod_x, 0, norm_ffn1[l], wf1_in, wf1_out)
        xc = _ffn_call(xc, mod_c, 0, norm_ffn1[l], wf1_in, wf1_out)

        proj_consts = (qg, kg, e_q, p_q, e_k, p_k, dn_conv[l], e_sum)
        xc_seq = xc.reshape(nb, lc, d)
        pool_x, qd_x, kd_x, vd_x, z_x, ba_x, aq_x, ak_x, av_x, gate_x = _proj_call(
            x, mod_x, 1, norm_mix[l], w_perm, cos_x, sin_x, *proj_consts)
        pool_c, qd_c, kd_c, vd_c, z_c, ba_c, aq_c, ak_c, av_c, gate_c = _proj_call(
            xc_seq, mod_c, 1, norm_mix[l], w_perm, cos_c, sin_c, *proj_consts)

        yd_c, yd_x = _delta_call((qd_c, kd_c, vd_c, ba_c, z_c), (qd_x, kd_x, vd_x, ba_x, z_x),
                                 alog, dtb, gn, e_mean, head_sel)

        k4_all, vt_all = _kv_layout(ak_c, av_c, ak_x, av_x)
        ya_x = _attn_call(aq_x, k4_all, vt_all)
        x = _merge_call(x, mod_x, 1, pool_x, yd_x, ya_x, gate_x, pool_bd, pool_s, w_br, w_o)

        if ctx_out:
            k4_c, vt_c = _kv_layout(ak_c, av_c)
            ya_c = _attn_call(aq_c, k4_c, vt_c)
            xc = _merge_call(xc_seq, mod_c, 1, pool_c, yd_c, ya_c, gate_c, pool_bd, pool_s, w_br, w_o)
            xc = xc.reshape(1, nb * lc, d)

        x = _ffn_call(x, mod_x, 2, norm_ffn2[l], wf2_in, wf2_out)
        if ctx_out:
            xc = _ffn_call(xc, mod_c, 2, norm_ffn2[l], wf2_in, wf2_out)

    return _final_norm_call(x, norm_final)
```

```python
import functools
import math

import jax
import jax.numpy as jnp
import numpy as np
from jax import lax
from jax.experimental import pallas as pl
from jax.experimental.pallas import tpu as pltpu

F32 = jnp.float32
BF16 = jnp.bfloat16

N_MOD = 9
FFN_RES = 0.5
POOL_WIDTH = 256
POOL_WINDOWS = (2, 4, 8, 16)
POOL_GROUP = 64
DN_HEADS = 4
DN_DK = 64
DN_DV = 64
DN_CONV = 5
DN_CHUNK = 64
DN_QK = DN_HEADS * DN_DK
DN_VW = DN_HEADS * DN_DV
DN_CONV_CH = 2 * DN_QK + DN_VW
ATT_HEADS = 8
ATT_KV = 2
ATT_GROUP = ATT_HEADS // ATT_KV
ATT_HD = 64
ROPE_BASE = 10000.0
GRID_W = 64
N_BRANCH = 3
EPS = 1e-6
LOG2E = 1.4426950408889634

LANE = 128
SUBLANE = 8
VMEM_LIMIT = 56 * 1024 * 1024

BA_PAD = LANE
Q_W = ATT_HEADS * ATT_HD
KV_W = ATT_KV * ATT_HD
V_ROWS = ATT_HD + 16
C_POOL = 0
C_QKV = C_POOL + POOL_WIDTH
C_Z = C_QKV + DN_CONV_CH
C_AQ = C_Z + DN_VW
C_AK = C_AQ + Q_W
C_AV = C_AK + KV_W
C_GATE = C_AV + KV_W


def _params(*sem):
    return pltpu.CompilerParams(dimension_semantics=sem, vmem_limit_bytes=VMEM_LIMIT)


def _tile(n, target):
    t = min(target, n)
    assert n % t == 0 and t % SUBLANE == 0, (n, t)
    return t


def _resident(shape):
    nd = len(shape)
    return pl.BlockSpec(shape, lambda *_: (0,) * nd, pipeline_mode=pl.Buffered(1))


def _layer_resident(stacked, layer):
    nd = stacked.ndim - 1
    return pl.BlockSpec((None,) + stacked.shape[1:], lambda *_: (layer,) + (0,) * nd,
                        pipeline_mode=pl.Buffered(1))


def _silu(x):
    return x * jax.nn.sigmoid(x)


def _dot(a, b):
    return jnp.dot(a, b, preferred_element_type=F32)


def _dot_nt(a, b):
    return lax.dot_general(a, b, (((1,), (1,)), ((), ())), preferred_element_type=F32)


def _dot_tn(a, b):
    return lax.dot_general(a, b, (((0,), (0,)), ((), ())), preferred_element_type=F32)


def _split3(x):
    hi = x.astype(BF16)
    r1 = x - hi.astype(F32)
    mid = r1.astype(BF16)
    lo = (r1 - mid.astype(F32)).astype(BF16)
    return hi, mid, lo


def _dot_exact_lhs(a_bf16, x_f32):
    hi, mid, lo = _split3(x_f32)
    return _dot(a_bf16, hi) + _dot(a_bf16, mid) + _dot(a_bf16, lo)


def _dot_exact_rhs(x_f32, b_bf16):
    hi, mid, lo = _split3(x_f32)
    return _dot(hi, b_bf16) + _dot(mid, b_bf16) + _dot(lo, b_bf16)


def _ada_kernel(s_ref, w_ref, b_ref, o_ref):
    s = _silu(s_ref[...])
    w = w_ref[0]
    s_hi, s_mid, s_lo = _split3(s)
    w_hi, w_mid, w_lo = _split3(w)
    acc = _dot(s_hi, w_hi)
    acc = acc + _dot(s_hi, w_mid) + _dot(s_mid, w_hi)
    acc = acc + _dot(s_hi, w_lo) + _dot(s_mid, w_mid) + _dot(s_lo, w_hi)
    o_ref[0] = acc + b_ref[0]


def _ada_call(cond, w_ada, b_ada):
    depth, d, n = w_ada.shape
    rows = cond.shape[0]
    tn = 1024
    return pl.pallas_call(
        _ada_kernel,
        out_shape=jax.ShapeDtypeStruct((depth, rows, n), F32),
        grid=(depth, n // tn),
        in_specs=[
            pl.BlockSpec((rows, d), lambda l, j: (0, 0)),
            pl.BlockSpec((1, d, tn), lambda l, j: (l, 0, j)),
            pl.BlockSpec((1, 1, tn), lambda l, j: (l, 0, j)),
        ],
        out_specs=pl.BlockSpec((1, rows, tn), lambda l, j: (l, 0, j)),
        compiler_params=_params("parallel", "parallel"),
        name="ada_mod",
    )(cond, w_ada, b_ada.reshape(depth, 1, n))


def _norm_mod(x, gain, shift, scale):
    ms = jnp.mean(x * x, axis=-1, keepdims=True)
    h = x * lax.rsqrt(ms + EPS) * gain
    return h * (1.0 + scale) + shift


def _ffn_kernel(x_ref, mod_ref, g_ref, wi_ref, wo_ref, og_ref, o_ref, acc_ref, *, d_ff, tf, out_norm):
    x = x_ref[0]
    h = _norm_mod(x, g_ref[...], mod_ref[0, 0], mod_ref[0, 1]).astype(BF16)
    for j in range(d_ff // tf):
        g = _dot(h, wi_ref[:, j * tf:(j + 1) * tf])
        u = _dot(h, wi_ref[:, d_ff + j * tf:d_ff + (j + 1) * tf])
        a = (_silu(g) * u).astype(BF16)
        part = _dot(a, wo_ref[j * tf:(j + 1) * tf, :])
        if j == 0:
            acc_ref[...] = part
        else:
            acc_ref[...] += part
    y = x + (FFN_RES * mod_ref[0, 2]) * acc_ref[...]
    if out_norm:
        y = y * lax.rsqrt(jnp.mean(y * y, axis=-1, keepdims=True) + EPS) * og_ref[...]
    o_ref[0] = y


def _ffn_call(x, mod, sub, gain, w_in, w_out, layer, out_gain, out_norm=False):
    nb, t, d = x.shape
    d_ff = w_out.shape[1]
    tm = _tile(t, 512)
    tf = 256
    kern = functools.partial(_ffn_kernel, d_ff=d_ff, tf=tf, out_norm=out_norm)
    return pl.pallas_call(
        kern,
        out_shape=jax.ShapeDtypeStruct(x.shape, F32),
        grid=(nb, t // tm),
        in_specs=[
            pl.BlockSpec((1, tm, d), lambda b, i: (b, i, 0)),
            pl.BlockSpec((1, 3, 1, d), lambda b, i: (b, sub, 0, 0)),
            _resident((1, d)),
            _layer_resident(w_in, layer),
            _layer_resident(w_out, layer),
            _resident((1, d)),
        ],
        out_specs=pl.BlockSpec((1, tm, d), lambda b, i: (b, i, 0)),
        scratch_shapes=[pltpu.VMEM((tm, d), F32)],
        compiler_params=_params("parallel", "parallel"),
        name="ffn",
    )(x, mod, gain.reshape(1, d), w_in, w_out, out_gain.reshape(1, d))


def _head_norm_rope(t_raw, gain, e_mat, p_mat, cos, sin, out_scale):
    ms = _dot((t_raw * t_raw).astype(BF16), e_mat)
    r = lax.rsqrt(ms + EPS) * out_scale
    t = t_raw * gain
    t_sw = _dot(t.astype(BF16), p_mat)
    w = t_raw.shape[1]
    outs = []
    for g in range(w // LANE):
        sl = slice(g * LANE, (g + 1) * LANE)
        outs.append(r[:, sl] * (t[:, sl] * cos + t_sw[:, sl] * sin))
    return jnp.concatenate(outs, axis=1) if len(outs) > 1 else outs[0]


def _proj_kernel(xp_ref, x_ref, xn_ref, mod_ref, g_ref, w_ref, cos_ref, sin_ref, qg_ref, kg_ref,
                 eq_ref, pq_ref, ek_ref, pk_ref, cw_ref, es_ref,
                 pool_ref, dq_ref, dk_ref, dv_ref, z_ref, ba_ref, q_ref, k_ref, v_ref, gate_ref,
                 pad_ref, *, n_gate, tm):
    gain = g_ref[...]
    shift = mod_ref[0, 0]
    scale = mod_ref[0, 1]
    h = _norm_mod(x_ref[0], gain, shift, scale).astype(BF16)

    def cols(lo, width):
        return _dot(h, w_ref[:, lo:lo + width])

    pool_ref[0] = cols(C_POOL, POOL_WIDTH)

    i = pl.program_id(1)
    n = pl.num_programs(1)
    halo = jnp.concatenate([xp_ref[0], xn_ref[0]], axis=0)
    h_halo = _norm_mod(halo, gain, shift, scale).astype(BF16)
    w_qkv = w_ref[:, C_QKV:C_QKV + DN_CONV_CH]
    qkv_halo = _dot(h_halo, w_qkv)
    pad_ref[0:SUBLANE, :] = jnp.where(i > 0, qkv_halo[0:SUBLANE], 0.0)
    pad_ref[SUBLANE + tm:2 * SUBLANE + tm, :] = jnp.where(i < n - 1, qkv_halo[SUBLANE:2 * SUBLANE], 0.0)
    for j in range(DN_CONV_CH // 256):
        pad_ref[SUBLANE:SUBLANE + tm, j * 256:(j + 1) * 256] = cols(C_QKV + j * 256, 256)
    z_ref[0] = cols(C_Z, DN_VW).astype(BF16)
    q_raw = cols(C_AQ, Q_W)
    k_raw = cols(C_AK, KV_W)
    v_ref[0] = cols(C_AV, KV_W).astype(BF16)
    tg = 512
    for j in range(n_gate // tg):
        gate_ref[0, :, j * tg:(j + 1) * tg] = jax.nn.sigmoid(cols(C_GATE + j * tg, tg)).astype(BF16)
    ba_ref[0] = cols(C_GATE + n_gate, BA_PAD)

    cos = cos_ref[...]
    sin = sin_ref[...]
    q = _head_norm_rope(q_raw, qg_ref[...], eq_ref[...], pq_ref[...], cos, sin, (ATT_HD ** -0.5) * LOG2E)
    q_ref[0] = q.astype(BF16)
    k = _head_norm_rope(k_raw, kg_ref[...], ek_ref[...], pk_ref[...], cos, sin, 1.0)
    k_ref[0] = k.astype(BF16)

    half = DN_CONV // 2
    n_pad = tm + 2 * SUBLANE
    for part, o_ref in enumerate((dq_ref, dk_ref, dv_ref)):
        cs = slice(part * DN_QK, (part + 1) * DN_QK)
        xp = pad_ref[:, cs]
        acc = None
        for j in range(DN_CONV):
            shifted = xp if j == half else pltpu.roll(xp, (half - j) % n_pad, 0)
            term = shifted[SUBLANE:SUBLANE + tm] * cw_ref[j:j + 1, cs]
            acc = term if acc is None else acc + term
        y = _silu(acc)
        if part < 2:
            y = y * lax.rsqrt(_dot((y * y).astype(BF16), es_ref[...]) + EPS)
            if part == 0:
                y = y * (DN_DK ** -0.5)
        o_ref[0] = y.astype(BF16)


def _mod_spec(mod, nb, sub, d):
    if mod.shape[0] == nb:
        return pl.BlockSpec((1, 3, 1, d), lambda b, i: (b, sub, 0, 0))
    return pl.BlockSpec((1, 3, 1, d), lambda b, i: (0, sub, 0, 0))


def _proj_call(x, mod, sub, gain, w_perm, layer, cos, sin, qg, kg, eq, pq, ek, pk, conv_w, e_sum):
    nb, t, d = x.shape
    n_gate = N_BRANCH * d
    tm = _tile(t, 512)
    nt = t // tm
    row = lambda w: pl.BlockSpec((1, tm, w), lambda b, i: (b, i, 0))
    widths = [(POOL_WIDTH, F32), (DN_QK, BF16), (DN_QK, BF16), (DN_VW, BF16), (DN_VW, BF16), (BA_PAD, F32),
              (Q_W, BF16), (KV_W, BF16), (KV_W, BF16), (n_gate, BF16)]
    kern = functools.partial(_proj_kernel, n_gate=n_gate, tm=tm)
    return pl.pallas_call(
        kern,
        out_shape=[jax.ShapeDtypeStruct((nb, t, wd), dt) for wd, dt in widths],
        grid=(nb, nt),
        in_specs=_halo_specs(tm, d, t) + [
            _mod_spec(mod, nb, sub, d),
            _resident((1, d)),
            _layer_resident(w_perm, layer),
            pl.BlockSpec((tm, LANE), lambda b, i: (i, 0)),
            pl.BlockSpec((tm, LANE), lambda b, i: (i, 0)),
            _resident((1, Q_W)),
            _resident((1, KV_W)),
            _resident((Q_W, Q_W)),
            _resident((Q_W, Q_W)),
            _resident((KV_W, KV_W)),
            _resident((KV_W, KV_W)),
            _resident(conv_w.shape),
            _resident(e_sum.shape),
        ],
        out_specs=[row(wd) for wd, _ in widths],
        scratch_shapes=[pltpu.VMEM((tm + 2 * SUBLANE, DN_CONV_CH), F32)],
        compiler_params=_params("parallel", "arbitrary"),
        name="mix_proj",
    )(x, x, x, mod, gain.reshape(1, d), w_perm, cos, sin, qg, kg, eq, pq, ek, pk, conv_w, e_sum)


def _fill_halo(pad_ref, prev_ref, cur_ref, next_ref, tt):
    i = pl.program_id(1)
    n = pl.num_programs(1)
    prev = prev_ref[0]
    nxt = next_ref[0]
    pad_ref[0:SUBLANE, :] = jnp.where(i > 0, prev, jnp.zeros_like(prev))
    pad_ref[SUBLANE:SUBLANE + tt, :] = cur_ref[0]
    pad_ref[SUBLANE + tt:2 * SUBLANE + tt, :] = jnp.where(i < n - 1, nxt, jnp.zeros_like(nxt))


def _halo_specs(tt, width, n_rows):
    per = tt // SUBLANE
    last = n_rows // SUBLANE - 1
    return [
        pl.BlockSpec((1, SUBLANE, width), lambda b, i: (b, jnp.maximum(i * per - 1, 0), 0)),
        pl.BlockSpec((1, tt, width), lambda b, i: (b, i, 0)),
        pl.BlockSpec((1, SUBLANE, width), lambda b, i: (b, jnp.minimum((i + 1) * per, last), 0)),
    ]


def _window_means_minus_self(pad_ref, tt, seq):
    assert POOL_WINDOWS == (2, 4, 8, 16) and POOL_GROUP * 2 == LANE
    i = pl.program_id(1)
    t = i * tt + lax.broadcasted_iota(jnp.int32, (tt, 1), 0)
    lane = lax.broadcasted_iota(jnp.int32, (1, LANE), 1)
    n = tt + 2 * SUBLANE
    body = slice(SUBLANE, SUBLANE + tt)

    def count(w):
        return (jnp.minimum(t + w - w // 2, seq) - jnp.maximum(t - w // 2, 0)).astype(F32)

    def ahead(a, k):
        return pltpu.roll(a, n - k, 0)

    def behind(a, k):
        return pltpu.roll(a, k, 0)

    tiles = []
    for lt in range(POOL_WIDTH // LANE):
        x = pad_ref[:, lt * LANE:(lt + 1) * LANE]
        r2 = x + ahead(x, 1)
        r4 = r2 + ahead(r2, 2)
        if lt == 0:
            narrow = behind(r2, 1)[body] / count(2)
            wide = behind(r4, 2)[body] / count(4)
        else:
            r8 = r4 + ahead(r4, 4)
            narrow = behind(r8, 4)[body] / count(8)
            wide = (r8[0:tt] + r8[body]) / count(16)
        tiles.append(jnp.where(lane >= POOL_GROUP, wide, narrow) - x[body])
    return jnp.concatenate(tiles, axis=1)


def _delta_kernel(qc_ref, kc_ref, vc_ref, bac_ref, zc_ref, qx_ref, kx_ref, vx_ref, bax_ref, zx_ref,
                  alog_ref, dtb_ref, gn_ref, e_ref, sel_ref, yc_ref, yx_ref,
                  sf_ref, sb_ref, ocf_ref, ocb_ref, oxf_ref, oxb_ref, *, n_ctx, n_lat, group):
    c = DN_CHUNK
    w = DN_HEADS * c
    row = lax.broadcasted_iota(jnp.int32, (w, w), 0)
    col = lax.broadcasted_iota(jnp.int32, (w, w), 1)
    rc_xor = row ^ col
    same = (rc_xor >> 6) == 0
    rp = row & (c - 1)
    cp = col & (c - 1)
    eye = jnp.where(row == col, 1.0, 0.0).astype(F32)
    m_incl = (same & (rp >= cp), same & (rp <= cp))
    m_strict = (same & (rp > cp), same & (rp < cp))
    r64 = lax.broadcasted_iota(jnp.int32, (c, c), 0)
    c64 = lax.broadcasted_iota(jnp.int32, (c, c), 1)
    cum_mat = (jnp.where(r64 >= c64, 1.0, 0.0).astype(BF16), jnp.where(r64 <= c64, 1.0, 0.0).astype(BF16))
    neg_a = -jnp.exp(alog_ref[...])
    dtb = dtb_ref[...]

    sf_ref[...] = jnp.zeros_like(sf_ref)
    sb_ref[...] = jnp.zeros_like(sb_ref)

    def spread(a):
        return jnp.where(same, jnp.concatenate([a, a, a, a], axis=0), 0.0)

    def gather_heads(o):
        return o[0:c] + o[c:2 * c] + o[2 * c:3 * c] + o[3 * c:4 * c]

    def prepare(refs, chunks):
        q_ref, k_ref, v_ref, ba_ref = refs
        st = []
        for rows, d in chunks:
            q = q_ref[0, rows, :].astype(F32)
            k = k_ref[0, rows, :].astype(F32)
            v = v_ref[0, rows, :].astype(F32)
            ba = ba_ref[0, rows, :]
            g_all = neg_a * jax.nn.softplus(ba + dtb)
            cum = _dot_exact_lhs(cum_mat[d], g_all)
            st.append(dict(d=d, q=q, k=k, v=v, ba=ba, cum=cum))
        for x in st:
            sel = sel_ref[x["d"]]
            x["beta"] = _dot(jax.nn.sigmoid(x["ba"]).astype(BF16), sel[:, :w])
            x["g_cum"] = _dot_exact_rhs(x["cum"], sel[:, w:])
        for x in st:
            x["kb"] = x["k"] * x["beta"]
            k_bd = spread(x["k"]).astype(BF16)
            x["prod"] = _dot_nt(jnp.concatenate([spread(x["kb"]), spread(x["q"])], axis=0).astype(BF16), k_bd)
        for x in st:
            d = x["d"]
            g_col = spread(x["g_cum"])
            decay = jnp.where(m_incl[d], jnp.exp(g_col - g_col.T), 0.0)
            a = jnp.where(m_strict[d], x["prod"][:w] * decay, 0.0)
            x["qkd"] = (x["prod"][w:] * decay).astype(BF16)
            x["a"] = a
            a4 = jnp.where((rc_xor >> 2) == 0, a, 0.0)
            x["t0"] = eye - a4
            x["a4"] = a4.astype(BF16)
        for x in st:
            x["a4sq"] = _dot(x["a4"], x["a4"]).astype(BF16)
        for x in st:
            x["t"] = x["t0"] + _dot(x["t0"].astype(BF16), x["a4sq"])
        for lg in range(2, 6):
            for x in st:
                b_ring = jnp.where((rc_xor >> lg) == 1, x["a"], 0.0).astype(BF16)
                x["t_b"] = x["t"].astype(BF16)
                x["tb"] = _dot(x["t_b"], b_ring).astype(BF16)
            for x in st:
                x["t"] = x["t"] - _dot(x["tb"], x["t_b"])
        out = []
        for x in st:
            last = c - 1 if x["d"] == 0 else 0
            g_cum = x["g_cum"]
            g_tot = g_cum[last:last + 1, :]
            e_g = jnp.exp(g_cum)
            rhs = jnp.concatenate([spread(x["v"] * x["beta"]), spread(x["kb"] * e_g)], axis=1).astype(BF16)
            uw = _dot(x["t"].astype(BF16), rhs)
            out.append(dict(u=uw[:, :w],
                            lhs=jnp.concatenate([uw[:, w:], spread(x["q"] * e_g)], axis=0).astype(BF16),
                            qkd=x["qkd"], kd=spread(x["k"] * jnp.exp(g_tot - g_cum)).astype(BF16),
                            e_tot=jnp.exp(g_tot)))
        return out

    def advance(pre, s_ref):
        s_old = s_ref[...]
        ws_qs = _dot(pre["lhs"], s_old.astype(BF16))
        v_new = (pre["u"] - ws_qs[:w]).astype(BF16)
        o = ws_qs[w:] + _dot(pre["qkd"], v_new)
        s_ref[...] = s_old * pre["e_tot"] + _dot_tn(pre["kd"], v_new)
        return gather_heads(o)

    def run_block(refs, of_ref, ob_ref, rows_f, rows_b):
        chunks = []
        for g in range(len(rows_f)):
            chunks += [(rows_f[g], 0), (rows_b[g], 1)]
        pre = prepare(refs, chunks)
        for g in range(len(rows_f)):
            of_ref[rows_f[g], :] = advance(pre[2 * g], sf_ref)
            ob_ref[rows_b[g], :] = advance(pre[2 * g + 1], sb_ref)

    ctx_refs = (qc_ref, kc_ref, vc_ref, bac_ref)
    lat_refs = (qx_ref, kx_ref, vx_ref, bax_ref)
    for j in range(n_ctx // group):
        run_block(ctx_refs, ocf_ref, ocb_ref,
                  [pl.ds((j * group + g) * c, c) for g in range(group)],
                  [pl.ds((n_ctx - 1 - j * group - g) * c, c) for g in range(group)])

    def lat_body(j, carry):
        rows_f = [pl.ds(pl.multiple_of((j * group + g) * c, c), c) for g in range(group)]
        rows_b = [pl.ds(pl.multiple_of((n_lat - 1 - j * group - g) * c, c), c) for g in range(group)]
        run_block(lat_refs, oxf_ref, oxb_ref, rows_f, rows_b)
        return carry

    lax.fori_loop(0, n_lat // group, lat_body, 0)

    def finish(of_ref, ob_ref, z_ref, y_ref, n_rows):
        tt = min(512, n_rows)
        for j in range(n_rows // tt):
            rs = slice(j * tt, (j + 1) * tt)
            o = of_ref[rs, :] + ob_ref[rs, :]
            ms = _dot((o * o).astype(BF16), e_ref[...])
            y = o * lax.rsqrt(ms + EPS) * gn_ref[...]
            y_ref[0, rs, :] = (y * _silu(z_ref[0, rs, :].astype(F32))).astype(BF16)

    finish(ocf_ref, ocb_ref, zc_ref, yc_ref, n_ctx * c)
    finish(oxf_ref, oxb_ref, zx_ref, yx_ref, n_lat * c)


def _delta_call(ctx_parts, lat_parts, alog, dtb, gn, e_mean, sel):
    qc, kc, vc, bac, zc = ctx_parts
    qx, kx, vx, bax, zx = lat_parts
    nb, lc, wd = qc.shape
    lx = qx.shape[1]
    group = 2
    assert lc % (group * DN_CHUNK) == 0 and lx % (group * DN_CHUNK) == 0, (lc, lx)
    assert lc % _tile(lc, 512) == 0 and lx % _tile(lx, 512) == 0
    kern = functools.partial(_delta_kernel, n_ctx=lc // DN_CHUNK, n_lat=lx // DN_CHUNK, group=group)
    seq = lambda n, width: pl.BlockSpec((1, n, width), lambda b: (b, 0, 0))
    in_specs = [seq(lc, wd), seq(lc, wd), seq(lc, wd), seq(lc, BA_PAD), seq(lc, wd),
                seq(lx, wd), seq(lx, wd), seq(lx, wd), seq(lx, BA_PAD), seq(lx, wd),
                _resident((1, LANE)), _resident((1, LANE)), _resident((1, wd)), _resident((wd, wd)),
                _resident(sel.shape)]
    return pl.pallas_call(
        kern,
        out_shape=[jax.ShapeDtypeStruct((nb, lc, wd), BF16), jax.ShapeDtypeStruct((nb, lx, wd), BF16)],
        grid=(nb,),
        in_specs=in_specs,
        out_specs=[seq(lc, wd), seq(lx, wd)],
        scratch_shapes=[pltpu.VMEM((wd, wd), F32), pltpu.VMEM((wd, wd), F32),
                        pltpu.VMEM((lc, wd), F32), pltpu.VMEM((lc, wd), F32),
                        pltpu.VMEM((lx, wd), F32), pltpu.VMEM((lx, wd), F32)],
        compiler_params=_params("parallel"),
        name="delta_rule",
    )(qc, kc, vc, bac, zc, qx, kx, vx, bax, zx, alog, dtb, gn, e_mean, sel)


P_CLAMP = 100.0
P_OVERFLOW = 2.0 ** 99


def _attn_kernel(q_ref, k_ref, vt_ref, o_ref, qs_ref, s_ref, p_ref, al_ref, m_ref, acc_ref, *, tq, tk, n_k):
    for h in range(ATT_GROUP):
        qs_ref[h * tq:(h + 1) * tq, :] = q_ref[0, :, h * ATT_HD:(h + 1) * ATT_HD]

    def keys(j):
        return k_ref[0, 0, j * tk:(j + 1) * tk, :]

    def vals(j):
        return vt_ref[0, 0, :, j * tk:(j + 1) * tk]

    s0 = _dot_nt(keys(0), qs_ref[...])
    m0 = jnp.max(s0, axis=0, keepdims=True)
    m_ref[...] = m0
    acc_ref[...] = _dot(vals(0), jnp.exp2(s0 - m0).astype(BF16))

    def probs(j, slot):
        s = _dot_nt(keys(j), qs_ref[...])
        p_ref[slot] = jnp.exp2(jnp.minimum(s - m_ref[...], P_CLAMP)).astype(BF16)

    def accumulate(j, slot):
        acc_ref[...] += _dot(vals(j), p_ref[slot])

    for i in range(1, n_k + 1):
        if i < n_k:
            probs(i, i % 2)
        if i >= 2:
            accumulate(i - 1, (i - 1) % 2)

    @pl.when(jnp.max(acc_ref[ATT_HD:ATT_HD + 1, :]) >= P_OVERFLOW)
    def _():
        m_ref[...] = jnp.full(m_ref.shape, -1e30, F32)
        acc_ref[...] = jnp.zeros(acc_ref.shape, F32)

        def scores(j, slot):
            s_ref[slot] = _dot_nt(keys(j), qs_ref[...])

        def softmax(slot):
            s = s_ref[slot]
            m_old = m_ref[...]
            m_new = jnp.maximum(m_old, jnp.max(s, axis=0, keepdims=True))
            p_ref[slot] = jnp.exp2(s - m_new).astype(BF16)
            al_ref[slot] = jnp.exp2(m_old - m_new)
            m_ref[...] = m_new

        def values(j, slot):
            acc_ref[...] = acc_ref[...] * al_ref[slot] + _dot(vals(j), p_ref[slot])

        for i in range(n_k + 2):
            if i < n_k:
                scores(i, i % 2)
            if 1 <= i <= n_k:
                softmax((i - 1) % 2)
            if i >= 2:
                values(i - 2, i % 2)

    acc = acc_ref[...]
    out = acc[:ATT_HD] / acc[ATT_HD:ATT_HD + 1]
    o_ref[0] = jnp.concatenate([out[:, h * tq:(h + 1) * tq].T for h in range(ATT_GROUP)], axis=1).astype(BF16)


def _attn_call(q, k4, vt):
    nb, lq, _ = q.shape
    s = k4.shape[2]
    tq = _tile(lq, 512)
    tk = 256
    assert s % tk == 0, s
    gw = ATT_GROUP * ATT_HD
    cols = ATT_GROUP * tq
    kern = functools.partial(_attn_kernel, tq=tq, tk=tk, n_k=s // tk)
    return pl.pallas_call(
        kern,
        out_shape=jax.ShapeDtypeStruct((nb, lq, Q_W), BF16),
        grid=(nb, ATT_KV, lq // tq),
        in_specs=[
            pl.BlockSpec((1, tq, gw), lambda b, g, i: (b, i, g)),
            pl.BlockSpec((1, 1, s, ATT_HD), lambda b, g, i: (b, g, 0, 0)),
            pl.BlockSpec((1, 1, V_ROWS, s), lambda b, g, i: (b, g, 0, 0)),
        ],
        out_specs=pl.BlockSpec((1, tq, gw), lambda b, g, i: (b, i, g)),
        scratch_shapes=[pltpu.VMEM((cols, ATT_HD), BF16), pltpu.VMEM((2, tk, cols), F32),
                        pltpu.VMEM((2, tk, cols), BF16), pltpu.VMEM((2, 1, cols), F32),
                        pltpu.VMEM((1, cols), F32), pltpu.VMEM((V_ROWS, cols), F32)],
        compiler_params=_params("parallel", "parallel", "arbitrary"),
        name="gqa_attn",
    )(q, k4, vt)


def _merge_kernel(pp_ref, pc_ref, pn_ref, x_ref, mod_ref, yd_ref, ya_ref, gate_ref, pw_ref, ps_ref,
                  wb_ref, wo_ref, o_ref, pad_ref, *, tm, seq):
    d = x_ref.shape[2]
    a0 = POOL_WIDTH
    a1 = POOL_WIDTH + DN_VW
    m = gate_ref[0, :, d:2 * d].astype(F32) * _dot(yd_ref[0], wb_ref[a0:a1, :])
    m = m + gate_ref[0, :, 2 * d:3 * d].astype(F32) * _dot(ya_ref[0], wb_ref[a1:, :])
    _fill_halo(pad_ref, pp_ref, pc_ref, pn_ref, tm)
    pooled = _window_means_minus_self(pad_ref, tm, seq)
    y_pool = (_dot(pooled.astype(BF16), pw_ref[...]) * ps_ref[...]).astype(BF16)
    m = m + gate_ref[0, :, 0:d].astype(F32) * _dot(y_pool, wb_ref[0:a0, :])
    o_ref[0] = x_ref[0] + mod_ref[0, 2] * _dot(m.astype(BF16), wo_ref[...])


def _merge_call(x, mod, sub, pool, y_dn, y_att, gate, pool_bd, pool_s, w_branch, w_out, layer):
    nb, t, d = x.shape
    tm = _tile(t, 512)
    row = lambda w: pl.BlockSpec((1, tm, w), lambda b, i: (b, i, 0))
    kern = functools.partial(_merge_kernel, tm=tm, seq=t)
    return pl.pallas_call(
        kern,
        out_shape=jax.ShapeDtypeStruct(x.shape, F32),
        grid=(nb, t // tm),
        in_specs=_halo_specs(tm, POOL_WIDTH, t) + [
            row(d), _mod_spec(mod, nb, sub, d), row(DN_VW), row(Q_W), row(N_BRANCH * d),
            _resident(pool_bd.shape), _resident(pool_s.shape),
            _layer_resident(w_branch, layer), _layer_resident(w_out, layer)],
        out_specs=row(d),
        scratch_shapes=[pltpu.VMEM((tm + 2 * SUBLANE, POOL_WIDTH), F32)],
        compiler_params=_params("parallel", "arbitrary"),
        name="merge",
    )(pool, pool, pool, x, mod, y_dn, y_att, gate, pool_bd, pool_s, w_branch, w_out)


def _block_diag_const(width, block, value):
    idx = np.arange(width)
    return jnp.asarray(np.where((idx[:, None] // block) == (idx[None, :] // block), value, 0.0), BF16)


def _pair_swap_const(width):
    idx = np.arange(width)
    return jnp.asarray((idx[:, None] == (idx[None, :] ^ 1)).astype(np.float32), BF16)


def _rope_tables(n_tokens):
    rows = n_tokens // GRID_W
    row = jnp.broadcast_to(jnp.arange(rows)[:, None], (rows, GRID_W)).reshape(-1)
    col = jnp.broadcast_to(jnp.arange(GRID_W)[None, :], (rows, GRID_W)).reshape(-1)
    n_freq = ATT_HD // 4
    inv = ROPE_BASE ** (-jnp.arange(n_freq, dtype=F32) / n_freq)
    ang = jnp.concatenate([row[:, None].astype(F32) * inv, col[:, None].astype(F32) * inv], axis=-1)
    cos = jnp.repeat(jnp.cos(ang), 2, axis=1)
    sin = jnp.repeat(jnp.sin(ang), 2, axis=1)
    sign = jnp.where(jnp.arange(ATT_HD) % 2 == 0, -1.0, 1.0).astype(F32)
    sin = sin * sign
    reps = LANE // ATT_HD
    return jnp.tile(cos, (1, reps)), jnp.tile(sin, (1, reps))


def _permute_w_in(w_in):
    o_pool = 0
    o_qkv = o_pool + POOL_WIDTH
    o_z = o_qkv + DN_CONV_CH
    o_ba = o_z + DN_VW
    o_aq = o_ba + 4 * DN_HEADS
    o_gate = o_aq + Q_W + 2 * KV_W
    pad = jnp.zeros(w_in.shape[:-1] + (BA_PAD - 4 * DN_HEADS,), w_in.dtype)
    parts = [w_in[..., o_pool:o_ba], w_in[..., o_aq:], w_in[..., o_ba:o_aq], pad]
    return jnp.concatenate(parts, axis=-1).astype(BF16)


def _head_select_const():
    sel = np.zeros((2, LANE, 2 * DN_VW), np.float32)
    for d in range(2):
        for h in range(DN_HEADS):
            sel[d, d * DN_HEADS + h, h * DN_DK:(h + 1) * DN_DK] = 1.0
            sel[d, 2 * DN_HEADS + d * DN_HEADS + h, DN_VW + h * DN_DK:DN_VW + (h + 1) * DN_DK] = 1.0
    return jnp.asarray(sel, BF16)


def _lane_vec(vals, offset):
    out = jnp.zeros((1, LANE), F32)
    return out.at[0, offset:offset + vals.size].set(vals.reshape(-1).astype(F32))


def _kv_layout(k_c, v_c, k_x=None, v_x=None):
    k = k_c if k_x is None else jnp.concatenate([k_c, k_x], axis=1)
    v = v_c if v_x is None else jnp.concatenate([v_c, v_x], axis=1)
    nb, s, _ = k.shape
    k4 = k.reshape(nb, s, ATT_KV, ATT_HD).transpose(0, 2, 1, 3)
    vt = v.reshape(nb, s, ATT_KV, ATT_HD).transpose(0, 2, 3, 1)
    ones = jnp.ones((nb, ATT_KV, 1, s), vt.dtype)
    zeros = jnp.zeros((nb, ATT_KV, V_ROWS - ATT_HD - 1, s), vt.dtype)
    return k4, jnp.concatenate([vt, ones, zeros], axis=2)


def kernel(x, c, ctx, c_ctx, w_ada, b_ada, norm_ffn1, w_ffn1_in, w_ffn1_out, norm_mix, w_in, pool_w, pool_scale, dn_conv, dn_a_log, dn_dt_bias, dn_norm, q_norm, k_norm, w_branch, w_out, norm_ffn2, w_ffn2_in, w_ffn2_out, norm_final):
    nb, seq, d = x.shape
    lc = ctx.shape[1]
    depth = w_ada.shape[0]

    n_rows = -(-(nb + 1) // SUBLANE) * SUBLANE
    cond = jnp.concatenate([c, c_ctx[None], jnp.zeros((n_rows - nb - 1, d), F32)], axis=0)
    mod_all = _ada_call(cond, w_ada, b_ada).reshape(depth, n_rows, N_MOD, 1, d)

    cos_x, sin_x = _rope_tables(seq)
    cos_c = jnp.ones((lc, LANE), F32)
    sin_c = jnp.zeros((lc, LANE), F32)
    e_q = _block_diag_const(Q_W, ATT_HD, 1.0 / ATT_HD)
    e_k = _block_diag_const(KV_W, ATT_HD, 1.0 / ATT_HD)
    p_q = _pair_swap_const(Q_W)
    p_k = _pair_swap_const(KV_W)
    e_sum = _block_diag_const(DN_QK, DN_DK, 1.0)
    e_mean = _block_diag_const(DN_VW, DN_DV, 1.0 / DN_DV)
    head_sel = _head_select_const()

    wf1_in = w_ffn1_in.astype(BF16)
    wf1_out = w_ffn1_out.astype(BF16)
    wf2_in = w_ffn2_in.astype(BF16)
    wf2_out = w_ffn2_out.astype(BF16)
    w_perm = _permute_w_in(w_in)
    w_br = w_branch.astype(BF16)
    w_o = w_out.astype(BF16)

    xc = ctx.reshape(1, nb * lc, d)
    for l in range(depth):
        ctx_out = l < depth - 1
        mod_x = mod_all[l, :nb]
        mod_c = mod_all[l, nb:nb + 1]
        qg = jnp.tile(q_norm[l], ATT_HEADS).reshape(1, Q_W)
        kg = jnp.tile(k_norm[l], ATT_KV).reshape(1, KV_W)
        pool_bd = jax.scipy.linalg.block_diag(*[pool_w[l, g] for g in range(len(POOL_WINDOWS))]).astype(BF16)
        pool_s = pool_scale[l].reshape(1, POOL_WIDTH)
        alog = _lane_vec(dn_a_log[l], 2 * DN_HEADS)
        dtb = _lane_vec(dn_dt_bias[l], 2 * DN_HEADS)
        gn = jnp.tile(dn_norm[l], DN_HEADS).reshape(1, DN_VW)

        x = _ffn_call(x, mod_x, 0, norm_ffn1[l], wf1_in, wf1_out, l, norm_final)
        xc = _ffn_call(xc, mod_c, 0, norm_ffn1[l], wf1_in, wf1_out, l, norm_final)

        proj_consts = (qg, kg, e_q, p_q, e_k, p_k, dn_conv[l], e_sum)
        xc_seq = xc.reshape(nb, lc, d)
        pool_x, qd_x, kd_x, vd_x, z_x, ba_x, aq_x, ak_x, av_x, gate_x = _proj_call(
            x, mod_x, 1, norm_mix[l], w_perm, l, cos_x, sin_x, *proj_consts)
        pool_c, qd_c, kd_c, vd_c, z_c, ba_c, aq_c, ak_c, av_c, gate_c = _proj_call(
            xc_seq, mod_c, 1, norm_mix[l], w_perm, l, cos_c, sin_c, *proj_consts)

        yd_c, yd_x = _delta_call((qd_c, kd_c, vd_c, ba_c, z_c), (qd_x, kd_x, vd_x, ba_x, z_x),
                                 alog, dtb, gn, e_mean, head_sel)

        k4_all, vt_all = _kv_layout(ak_c, av_c, ak_x, av_x)
        ya_x = _attn_call(aq_x, k4_all, vt_all)
        x = _merge_call(x, mod_x, 1, pool_x, yd_x, ya_x, gate_x, pool_bd, pool_s, w_br, w_o, l)

        if ctx_out:
            k4_c, vt_c = _kv_layout(ak_c, av_c)
            ya_c = _attn_call(aq_c, k4_c, vt_c)
            xc = _merge_call(xc_seq, mod_c, 1, pool_c, yd_c, ya_c, gate_c, pool_bd, pool_s, w_br, w_o, l)
            xc = xc.reshape(1, nb * lc, d)

        x = _ffn_call(x, mod_x, 2, norm_ffn2[l], wf2_in, wf2_out, l, norm_final, out_norm=not ctx_out)
        if ctx_out:
            xc = _ffn_call(xc, mod_c, 2, norm_ffn2[l], wf2_in, wf2_out, l, norm_final)

    return x
```

```python
import functools

import jax
import jax.numpy as jnp
import numpy as np
from jax import lax
from jax.experimental import pallas as pl
from jax.experimental.pallas import tpu as pltpu

F32 = jnp.float32
BF16 = jnp.bfloat16

N_MOD = 9
FFN_RES = 0.5
POOL_WIDTH = 256
POOL_WINDOWS = (2, 4, 8, 16)
POOL_GROUP = 64
DN_HEADS = 4
DN_DK = 64
DN_DV = 64
DN_CONV = 5
DN_CHUNK = 64
DN_QK = DN_HEADS * DN_DK
DN_VW = DN_HEADS * DN_DV
DN_CONV_CH = 2 * DN_QK + DN_VW
ATT_HEADS = 8
ATT_KV = 2
ATT_GROUP = ATT_HEADS // ATT_KV
ATT_HD = 64
ROPE_BASE = 10000.0
GRID_W = 64
N_BRANCH = 3
EPS = 1e-6
LOG2E = 1.4426950408889634

LANE = 128
SUBLANE = 8
VMEM_LIMIT = 56 * 1024 * 1024

BA_PAD = LANE
Q_W = ATT_HEADS * ATT_HD
KV_W = ATT_KV * ATT_HD
V_ROWS = ATT_HD + 16
C_POOL = 0
C_QKV = C_POOL + POOL_WIDTH
C_Z = C_QKV + DN_CONV_CH
C_AQ = C_Z + DN_VW
C_AK = C_AQ + Q_W
C_AV = C_AK + KV_W
C_GATE = C_AV + KV_W


def _params(*sem):
    return pltpu.CompilerParams(dimension_semantics=sem, vmem_limit_bytes=VMEM_LIMIT)


def _tile(n, target):
    t = min(target, n)
    assert n % t == 0 and t % SUBLANE == 0, (n, t)
    return t


def _resident(shape):
    nd = len(shape)
    return pl.BlockSpec(shape, lambda *_: (0,) * nd, pipeline_mode=pl.Buffered(1))


def _layer_resident(stacked, layer):
    nd = stacked.ndim - 1
    return pl.BlockSpec((None,) + stacked.shape[1:], lambda *_: (layer,) + (0,) * nd,
                        pipeline_mode=pl.Buffered(1))


def _silu(x):
    return x * jax.nn.sigmoid(x)


def _dot(a, b):
    return jnp.dot(a, b, preferred_element_type=F32)


def _dot_nt(a, b):
    return lax.dot_general(a, b, (((1,), (1,)), ((), ())), preferred_element_type=F32)


def _dot_tn(a, b):
    return lax.dot_general(a, b, (((0,), (0,)), ((), ())), preferred_element_type=F32)


def _split3(x):
    hi = x.astype(BF16)
    r1 = x - hi.astype(F32)
    mid = r1.astype(BF16)
    lo = (r1 - mid.astype(F32)).astype(BF16)
    return hi, mid, lo


def _dot_exact_lhs(a_bf16, x_f32):
    hi, mid, lo = _split3(x_f32)
    return _dot(a_bf16, hi) + _dot(a_bf16, mid) + _dot(a_bf16, lo)


def _dot_exact_rhs(x_f32, b_bf16):
    hi, mid, lo = _split3(x_f32)
    return _dot(hi, b_bf16) + _dot(mid, b_bf16) + _dot(lo, b_bf16)


def _ada_kernel(s_ref, w_ref, b_ref, o_ref):
    s = _silu(s_ref[...])
    w = w_ref[0]
    s_hi, s_mid, s_lo = _split3(s)
    w_hi, w_mid, w_lo = _split3(w)
    acc = _dot(s_hi, w_hi)
    acc = acc + _dot(s_hi, w_mid) + _dot(s_mid, w_hi)
    acc = acc + _dot(s_hi, w_lo) + _dot(s_mid, w_mid) + _dot(s_lo, w_hi)
    o_ref[0] = acc + b_ref[0]


def _ada_call(cond, w_ada, b_ada):
    depth, d, n = w_ada.shape
    rows = cond.shape[0]
    tn = 1024
    return pl.pallas_call(
        _ada_kernel,
        out_shape=jax.ShapeDtypeStruct((depth, rows, n), F32),
        grid=(depth, n // tn),
        in_specs=[
            pl.BlockSpec((rows, d), lambda l, j: (0, 0)),
            pl.BlockSpec((1, d, tn), lambda l, j: (l, 0, j)),
            pl.BlockSpec((1, 1, tn), lambda l, j: (l, 0, j)),
        ],
        out_specs=pl.BlockSpec((1, rows, tn), lambda l, j: (l, 0, j)),
        compiler_params=_params("parallel", "parallel"),
        name="ada_mod",
    )(cond, w_ada, b_ada.reshape(depth, 1, n))


def _norm_mod(x, gain, shift, scale):
    ms = jnp.mean(x * x, axis=-1, keepdims=True)
    h = x * lax.rsqrt(ms + EPS) * gain
    return h * (1.0 + scale) + shift


def _ffn_kernel(x_ref, mod_ref, g_ref, wi_ref, wo_ref, og_ref, o_ref, acc_ref, *, d_ff, tf, out_norm):
    x = x_ref[0]
    h = _norm_mod(x, g_ref[...], mod_ref[0, 0], mod_ref[0, 1]).astype(BF16)
    for j in range(d_ff // tf):
        g = _dot(h, wi_ref[:, j * tf:(j + 1) * tf])
        u = _dot(h, wi_ref[:, d_ff + j * tf:d_ff + (j + 1) * tf])
        a = (_silu(g) * u).astype(BF16)
        part = _dot(a, wo_ref[j * tf:(j + 1) * tf, :])
        if j == 0:
            acc_ref[...] = part
        else:
            acc_ref[...] += part
    y = x + (FFN_RES * mod_ref[0, 2]) * acc_ref[...]
    if out_norm:
        y = y * lax.rsqrt(jnp.mean(y * y, axis=-1, keepdims=True) + EPS) * og_ref[...]
    o_ref[0] = y


def _ffn_call(x, mod, sub, gain, w_in, w_out, layer, out_gain, out_norm=False):
    nb, t, d = x.shape
    d_ff = w_out.shape[1]
    tm = _tile(t, 512)
    tf = 256
    kern = functools.partial(_ffn_kernel, d_ff=d_ff, tf=tf, out_norm=out_norm)
    return pl.pallas_call(
        kern,
        out_shape=jax.ShapeDtypeStruct(x.shape, F32),
        grid=(nb, t // tm),
        in_specs=[
            pl.BlockSpec((1, tm, d), lambda b, i: (b, i, 0)),
            pl.BlockSpec((1, 3, 1, d), lambda b, i: (b, sub, 0, 0)),
            _resident((1, d)),
            _layer_resident(w_in, layer),
            _layer_resident(w_out, layer),
            _resident((1, d)),
        ],
        out_specs=pl.BlockSpec((1, tm, d), lambda b, i: (b, i, 0)),
        scratch_shapes=[pltpu.VMEM((tm, d), F32)],
        compiler_params=_params("parallel", "parallel"),
        name="ffn",
    )(x, mod, gain.reshape(1, d), w_in, w_out, out_gain.reshape(1, d))


def _head_norm_rope(t_raw, gain, e_mat, p_mat, cos, sin, out_scale):
    ms = _dot((t_raw * t_raw).astype(BF16), e_mat)
    r = lax.rsqrt(ms + EPS) * out_scale
    t = t_raw * gain
    t_sw = _dot(t.astype(BF16), p_mat)
    w = t_raw.shape[1]
    outs = []
    for g in range(w // LANE):
        sl = slice(g * LANE, (g + 1) * LANE)
        outs.append(r[:, sl] * (t[:, sl] * cos + t_sw[:, sl] * sin))
    return jnp.concatenate(outs, axis=1) if len(outs) > 1 else outs[0]


def _proj_kernel(xp_ref, x_ref, xn_ref, mod_ref, g_ref, w_ref, cos_ref, sin_ref, qg_ref, kg_ref,
                 eq_ref, pq_ref, ek_ref, pk_ref, cw_ref, es_ref,
                 pool_ref, dq_ref, dk_ref, dv_ref, z_ref, ba_ref, q_ref, k_ref, v_ref, gate_ref,
                 pad_ref, *, n_gate, tm):
    gain = g_ref[...]
    shift = mod_ref[0, 0]
    scale = mod_ref[0, 1]
    h = _norm_mod(x_ref[0], gain, shift, scale).astype(BF16)

    def cols(lo, width):
        return _dot(h, w_ref[:, lo:lo + width])

    pool_ref[0] = cols(C_POOL, POOL_WIDTH)

    i = pl.program_id(1)
    n = pl.num_programs(1)
    halo = jnp.concatenate([xp_ref[0], xn_ref[0]], axis=0)
    h_halo = _norm_mod(halo, gain, shift, scale).astype(BF16)
    w_qkv = w_ref[:, C_QKV:C_QKV + DN_CONV_CH]
    qkv_halo = _dot(h_halo, w_qkv)
    pad_ref[0:SUBLANE, :] = jnp.where(i > 0, qkv_halo[0:SUBLANE], 0.0)
    pad_ref[SUBLANE + tm:2 * SUBLANE + tm, :] = jnp.where(i < n - 1, qkv_halo[SUBLANE:2 * SUBLANE], 0.0)
    for j in range(DN_CONV_CH // 256):
        pad_ref[SUBLANE:SUBLANE + tm, j * 256:(j + 1) * 256] = cols(C_QKV + j * 256, 256)
    z_ref[0] = cols(C_Z, DN_VW).astype(BF16)
    q_raw = cols(C_AQ, Q_W)
    k_raw = cols(C_AK, KV_W)
    v_ref[0] = cols(C_AV, KV_W).astype(BF16)
    tg = 512
    for j in range(n_gate // tg):
        gate_ref[0, :, j * tg:(j + 1) * tg] = jax.nn.sigmoid(cols(C_GATE + j * tg, tg)).astype(BF16)
    ba_ref[0] = cols(C_GATE + n_gate, BA_PAD)

    cos = cos_ref[...]
    sin = sin_ref[...]
    q = _head_norm_rope(q_raw, qg_ref[...], eq_ref[...], pq_ref[...], cos, sin, (ATT_HD ** -0.5) * LOG2E)
    q_ref[0] = q.astype(BF16)
    k = _head_norm_rope(k_raw, kg_ref[...], ek_ref[...], pk_ref[...], cos, sin, 1.0)
    k_ref[0] = k.astype(BF16)

    half = DN_CONV // 2
    n_pad = tm + 2 * SUBLANE
    for part, o_ref in enumerate((dq_ref, dk_ref, dv_ref)):
        cs = slice(part * DN_QK, (part + 1) * DN_QK)
        xp = pad_ref[:, cs]
        acc = None
        for j in range(DN_CONV):
            shifted = xp if j == half else pltpu.roll(xp, (half - j) % n_pad, 0)
            term = shifted[SUBLANE:SUBLANE + tm] * cw_ref[j:j + 1, cs]
            acc = term if acc is None else acc + term
        y = _silu(acc)
        if part < 2:
            y = y * lax.rsqrt(_dot((y * y).astype(BF16), es_ref[...]) + EPS)
            if part == 0:
                y = y * (DN_DK ** -0.5)
        o_ref[0] = y.astype(BF16)


def _mod_spec(mod, nb, sub, d):
    if mod.shape[0] == nb:
        return pl.BlockSpec((1, 3, 1, d), lambda b, i: (b, sub, 0, 0))
    return pl.BlockSpec((1, 3, 1, d), lambda b, i: (0, sub, 0, 0))


def _proj_call(x, mod, sub, gain, w_perm, layer, cos, sin, qg, kg, eq, pq, ek, pk, conv_w, e_sum):
    nb, t, d = x.shape
    n_gate = N_BRANCH * d
    tm = _tile(t, 512)
    nt = t // tm
    row = lambda w: pl.BlockSpec((1, tm, w), lambda b, i: (b, i, 0))
    widths = [(POOL_WIDTH, F32), (DN_QK, BF16), (DN_QK, BF16), (DN_VW, BF16), (DN_VW, BF16), (BA_PAD, F32),
              (Q_W, BF16), (KV_W, BF16), (KV_W, BF16), (n_gate, BF16)]
    kern = functools.partial(_proj_kernel, n_gate=n_gate, tm=tm)
    return pl.pallas_call(
        kern,
        out_shape=[jax.ShapeDtypeStruct((nb, t, wd), dt) for wd, dt in widths],
        grid=(nb, nt),
        in_specs=_halo_specs(tm, d, t) + [
            _mod_spec(mod, nb, sub, d),
            _resident((1, d)),
            _layer_resident(w_perm, layer),
            pl.BlockSpec((tm, LANE), lambda b, i: (i, 0)),
            pl.BlockSpec((tm, LANE), lambda b, i: (i, 0)),
            _resident((1, Q_W)),
            _resident((1, KV_W)),
            _resident((Q_W, Q_W)),
            _resident((Q_W, Q_W)),
            _resident((KV_W, KV_W)),
            _resident((KV_W, KV_W)),
            _resident(conv_w.shape),
            _resident(e_sum.shape),
        ],
        out_specs=[row(wd) for wd, _ in widths],
        scratch_shapes=[pltpu.VMEM((tm + 2 * SUBLANE, DN_CONV_CH), F32)],
        compiler_params=_params("parallel", "arbitrary"),
        name="mix_proj",
    )(x, x, x, mod, gain.reshape(1, d), w_perm, cos, sin, qg, kg, eq, pq, ek, pk, conv_w, e_sum)


def _fill_halo(pad_ref, prev_ref, cur_ref, next_ref, tt):
    i = pl.program_id(1)
    n = pl.num_programs(1)
    prev = prev_ref[0]
    nxt = next_ref[0]
    pad_ref[0:SUBLANE, :] = jnp.where(i > 0, prev, jnp.zeros_like(prev))
    pad_ref[SUBLANE:SUBLANE + tt, :] = cur_ref[0]
    pad_ref[SUBLANE + tt:2 * SUBLANE + tt, :] = jnp.where(i < n - 1, nxt, jnp.zeros_like(nxt))


def _halo_specs(tt, width, n_rows):
    per = tt // SUBLANE
    last = n_rows // SUBLANE - 1
    return [
        pl.BlockSpec((1, SUBLANE, width), lambda b, i: (b, jnp.maximum(i * per - 1, 0), 0)),
        pl.BlockSpec((1, tt, width), lambda b, i: (b, i, 0)),
        pl.BlockSpec((1, SUBLANE, width), lambda b, i: (b, jnp.minimum((i + 1) * per, last), 0)),
    ]


def _window_means_minus_self(pad_ref, tt, seq):
    assert POOL_WINDOWS == (2, 4, 8, 16) and POOL_GROUP * 2 == LANE
    i = pl.program_id(1)
    t = i * tt + lax.broadcasted_iota(jnp.int32, (tt, 1), 0)
    lane = lax.broadcasted_iota(jnp.int32, (1, LANE), 1)
    n = tt + 2 * SUBLANE
    body = slice(SUBLANE, SUBLANE + tt)

    def count(w):
        return (jnp.minimum(t + w - w // 2, seq) - jnp.maximum(t - w // 2, 0)).astype(F32)

    def ahead(a, k):
        return pltpu.roll(a, n - k, 0)

    def behind(a, k):
        return pltpu.roll(a, k, 0)

    tiles = []
    for lt in range(POOL_WIDTH // LANE):
        x = pad_ref[:, lt * LANE:(lt + 1) * LANE]
        r2 = x + ahead(x, 1)
        r4 = r2 + ahead(r2, 2)
        if lt == 0:
            narrow = behind(r2, 1)[body] / count(2)
            wide = behind(r4, 2)[body] / count(4)
        else:
            r8 = r4 + ahead(r4, 4)
            narrow = behind(r8, 4)[body] / count(8)
            wide = (r8[0:tt] + r8[body]) / count(16)
        tiles.append(jnp.where(lane >= POOL_GROUP, wide, narrow) - x[body])
    return jnp.concatenate(tiles, axis=1)


def _delta_kernel(qc_ref, kc_ref, vc_ref, bac_ref, zc_ref, qx_ref, kx_ref, vx_ref, bax_ref, zx_ref,
                  alog_ref, dtb_ref, gn_ref, e_ref, sel_ref, yc_ref, yx_ref,
                  sf_ref, sb_ref, ocf_ref, ocb_ref, oxf_ref, oxb_ref, *, n_ctx, n_lat, group):
    c = DN_CHUNK
    w = DN_HEADS * c
    row = lax.broadcasted_iota(jnp.int32, (w, w), 0)
    col = lax.broadcasted_iota(jnp.int32, (w, w), 1)
    rc_xor = row ^ col
    same = (rc_xor >> 6) == 0
    rp = row & (c - 1)
    cp = col & (c - 1)
    eye = jnp.where(row == col, 1.0, 0.0).astype(F32)
    m_incl = (same & (rp >= cp), same & (rp <= cp))
    m_strict = (same & (rp > cp), same & (rp < cp))
    r64 = lax.broadcasted_iota(jnp.int32, (c, c), 0)
    c64 = lax.broadcasted_iota(jnp.int32, (c, c), 1)
    cum_mat = (jnp.where(r64 >= c64, 1.0, 0.0).astype(BF16), jnp.where(r64 <= c64, 1.0, 0.0).astype(BF16))
    neg_a = -jnp.exp(alog_ref[...])
    dtb = dtb_ref[...]

    sf_ref[...] = jnp.zeros_like(sf_ref)
    sb_ref[...] = jnp.zeros_like(sb_ref)

    def spread(a):
        return jnp.where(same, jnp.concatenate([a, a, a, a], axis=0), 0.0)

    def gather_heads(o):
        return o[0:c] + o[c:2 * c] + o[2 * c:3 * c] + o[3 * c:4 * c]

    def prepare(refs, chunks):
        q_ref, k_ref, v_ref, ba_ref = refs
        st = []
        for rows, d in chunks:
            q = q_ref[0, rows, :].astype(F32)
            k = k_ref[0, rows, :].astype(F32)
            v = v_ref[0, rows, :].astype(F32)
            ba = ba_ref[0, rows, :]
            g_all = neg_a * jax.nn.softplus(ba + dtb)
            cum = _dot_exact_lhs(cum_mat[d], g_all)
            st.append(dict(d=d, q=q, k=k, v=v, ba=ba, cum=cum))
        for x in st:
            sel = sel_ref[x["d"]]
            x["beta"] = _dot(jax.nn.sigmoid(x["ba"]).astype(BF16), sel[:, :w])
            x["g_cum"] = _dot_exact_rhs(x["cum"], sel[:, w:])
        for x in st:
            x["kb"] = x["k"] * x["beta"]
            k_bd = spread(x["k"]).astype(BF16)
            x["prod"] = _dot_nt(jnp.concatenate([spread(x["kb"]), spread(x["q"])], axis=0).astype(BF16), k_bd)
        for x in st:
            d = x["d"]
            g_col = spread(x["g_cum"])
            decay = jnp.where(m_incl[d], jnp.exp(g_col - g_col.T), 0.0)
            a = jnp.where(m_strict[d], x["prod"][:w] * decay, 0.0)
            x["qkd"] = (x["prod"][w:] * decay).astype(BF16)
            x["a"] = a
            a4 = jnp.where((rc_xor >> 2) == 0, a, 0.0)
            x["t0"] = eye - a4
            x["a4"] = a4.astype(BF16)
        for x in st:
            x["a4sq"] = _dot(x["a4"], x["a4"]).astype(BF16)
        for x in st:
            x["t"] = x["t0"] + _dot(x["t0"].astype(BF16), x["a4sq"])
        for lg in range(2, 6):
            for x in st:
                b_ring = jnp.where((rc_xor >> lg) == 1, x["a"], 0.0).astype(BF16)
                x["t_b"] = x["t"].astype(BF16)
                x["tb"] = _dot(x["t_b"], b_ring).astype(BF16)
            for x in st:
                x["t"] = x["t"] - _dot(x["tb"], x["t_b"])
        out = []
        for x in st:
            last = c - 1 if x["d"] == 0 else 0
            g_cum = x["g_cum"]
            g_tot = g_cum[last:last + 1, :]
            e_g = jnp.exp(g_cum)
            rhs = jnp.concatenate([spread(x["v"] * x["beta"]), spread(x["kb"] * e_g)], axis=1).astype(BF16)
            uw = _dot(x["t"].astype(BF16), rhs)
            out.append(dict(u=uw[:, :w],
                            lhs=jnp.concatenate([uw[:, w:], spread(x["q"] * e_g)], axis=0).astype(BF16),
                            qkd=x["qkd"], kd=spread(x["k"] * jnp.exp(g_tot - g_cum)).astype(BF16),
                            e_tot=jnp.exp(g_tot)))
        return out

    def advance(pre, s_ref):
        s_old = s_ref[...]
        ws_qs = _dot(pre["lhs"], s_old.astype(BF16))
        v_new = (pre["u"] - ws_qs[:w]).astype(BF16)
        o = ws_qs[w:] + _dot(pre["qkd"], v_new)
        s_ref[...] = s_old * pre["e_tot"] + _dot_tn(pre["kd"], v_new)
        return gather_heads(o)

    def run_block(refs, of_ref, ob_ref, rows_f, rows_b):
        chunks = []
        for g in range(len(rows_f)):
            chunks += [(rows_f[g], 0), (rows_b[g], 1)]
        pre = prepare(refs, chunks)
        for g in range(len(rows_f)):
            of_ref[rows_f[g], :] = advance(pre[2 * g], sf_ref)
            ob_ref[rows_b[g], :] = advance(pre[2 * g + 1], sb_ref)

    ctx_refs = (qc_ref, kc_ref, vc_ref, bac_ref)
    lat_refs = (qx_ref, kx_ref, vx_ref, bax_ref)
    for j in range(n_ctx // group):
        run_block(ctx_refs, ocf_ref, ocb_ref,
                  [pl.ds((j * group + g) * c, c) for g in range(group)],
                  [pl.ds((n_ctx - 1 - j * group - g) * c, c) for g in range(group)])

    def lat_body(j, carry):
        rows_f = [pl.ds(pl.multiple_of((j * group + g) * c, c), c) for g in range(group)]
        rows_b = [pl.ds(pl.multiple_of((n_lat - 1 - j * group - g) * c, c), c) for g in range(group)]
        run_block(lat_refs, oxf_ref, oxb_ref, rows_f, rows_b)
        return carry

    lax.fori_loop(0, n_lat // group, lat_body, 0)

    def finish(of_ref, ob_ref, z_ref, y_ref, n_rows):
        tt = min(512, n_rows)
        for j in range(n_rows // tt):
            rs = slice(j * tt, (j + 1) * tt)
            o = of_ref[rs, :] + ob_ref[rs, :]
            ms = _dot((o * o).astype(BF16), e_ref[...])
            y = o * lax.rsqrt(ms + EPS) * gn_ref[...]
            y_ref[0, rs, :] = (y * _silu(z_ref[0, rs, :].astype(F32))).astype(BF16)

    finish(ocf_ref, ocb_ref, zc_ref, yc_ref, n_ctx * c)
    finish(oxf_ref, oxb_ref, zx_ref, yx_ref, n_lat * c)


def _delta_call(ctx_parts, lat_parts, alog, dtb, gn, e_mean, sel):
    qc, kc, vc, bac, zc = ctx_parts
    qx, kx, vx, bax, zx = lat_parts
    nb, lc, wd = qc.shape
    lx = qx.shape[1]
    group = 2
    assert lc % (group * DN_CHUNK) == 0 and lx % (group * DN_CHUNK) == 0, (lc, lx)
    assert lc % _tile(lc, 512) == 0 and lx % _tile(lx, 512) == 0
    kern = functools.partial(_delta_kernel, n_ctx=lc // DN_CHUNK, n_lat=lx // DN_CHUNK, group=group)
    seq = lambda n, width: pl.BlockSpec((1, n, width), lambda b: (b, 0, 0))
    in_specs = [seq(lc, wd), seq(lc, wd), seq(lc, wd), seq(lc, BA_PAD), seq(lc, wd),
                seq(lx, wd), seq(lx, wd), seq(lx, wd), seq(lx, BA_PAD), seq(lx, wd),
                _resident((1, LANE)), _resident((1, LANE)), _resident((1, wd)), _resident((wd, wd)),
                _resident(sel.shape)]
    return pl.pallas_call(
        kern,
        out_shape=[jax.ShapeDtypeStruct((nb, lc, wd), BF16), jax.ShapeDtypeStruct((nb, lx, wd), BF16)],
        grid=(nb,),
        in_specs=in_specs,
        out_specs=[seq(lc, wd), seq(lx, wd)],
        scratch_shapes=[pltpu.VMEM((wd, wd), F32), pltpu.VMEM((wd, wd), F32),
                        pltpu.VMEM((lc, wd), F32), pltpu.VMEM((lc, wd), F32),
                        pltpu.VMEM((lx, wd), F32), pltpu.VMEM((lx, wd), F32)],
        compiler_params=_params("parallel"),
        name="delta_rule",
    )(qc, kc, vc, bac, zc, qx, kx, vx, bax, zx, alog, dtb, gn, e_mean, sel)


P_CLAMP = 100.0
P_OVERFLOW = 2.0 ** 99


def _attn_kernel(q_ref, kc_ref, vc_ref, kx_ref, vx_ref, o_ref, qs_ref, kh_ref, vt_ref, s_ref, p_ref, al_ref, m_ref,
                 acc_ref, *, tq, tk, n_k):
    for h in range(ATT_GROUP):
        qs_ref[h * tq:(h + 1) * tq, :] = q_ref[0, :, h * ATT_HD:(h + 1) * ATT_HD]

    @pl.when(pl.program_id(2) == 0)
    def _():
        head = pl.program_id(1)
        pieces = [(kc_ref, vc_ref, 0, tk)] + ([(kx_ref, vx_ref, tk, (n_k - 1) * tk)] if n_k > 1 else [])
        for k_src, v_src, lo, n in pieces:
            k_all = k_src[0, 0:n, :]
            kh_ref[lo:lo + n, :] = jnp.where(head == 0, k_all[:, 0:ATT_HD], k_all[:, ATT_HD:2 * ATT_HD])
            v_t = v_src[0, 0:n, :].astype(F32).T
            vt_ref[0:ATT_HD, lo:lo + n] = jnp.where(head == 0, v_t[0:ATT_HD], v_t[ATT_HD:2 * ATT_HD]).astype(BF16)
        tail = lax.broadcasted_iota(jnp.int32, (V_ROWS - ATT_HD, n_k * tk), 0) == 0
        vt_ref[ATT_HD:V_ROWS, :] = jnp.where(tail, 1.0, 0.0).astype(BF16)

    def keys(j):
        return kh_ref[j * tk:(j + 1) * tk, :]

    def vals(j):
        return vt_ref[:, j * tk:(j + 1) * tk]

    s0 = _dot_nt(keys(0), qs_ref[...])
    m0 = jnp.max(s0, axis=0, keepdims=True)
    m_ref[...] = m0
    acc_ref[...] = _dot(vals(0), jnp.exp2(s0 - m0).astype(BF16))

    def probs(j, slot):
        s = _dot_nt(keys(j), qs_ref[...])
        p_ref[slot] = jnp.exp2(jnp.minimum(s - m_ref[...], P_CLAMP)).astype(BF16)

    def accumulate(j, slot):
        acc_ref[...] += _dot(vals(j), p_ref[slot])

    for i in range(1, n_k + 1):
        if i < n_k:
            probs(i, i % 2)
        if i >= 2:
            accumulate(i - 1, (i - 1) % 2)

    @pl.when(jnp.max(acc_ref[ATT_HD:ATT_HD + 1, :]) >= P_OVERFLOW)
    def _():
        m_ref[...] = jnp.full(m_ref.shape, -1e30, F32)
        acc_ref[...] = jnp.zeros(acc_ref.shape, F32)

        def scores(j, slot):
            s_ref[slot] = _dot_nt(keys(j), qs_ref[...])

        def softmax(slot):
            s = s_ref[slot]
            m_old = m_ref[...]
            m_new = jnp.maximum(m_old, jnp.max(s, axis=0, keepdims=True))
            p_ref[slot] = jnp.exp2(s - m_new).astype(BF16)
            al_ref[slot] = jnp.exp2(m_old - m_new)
            m_ref[...] = m_new

        def values(j, slot):
            acc_ref[...] = acc_ref[...] * al_ref[slot] + _dot(vals(j), p_ref[slot])

        for i in range(n_k + 2):
            if i < n_k:
                scores(i, i % 2)
            if 1 <= i <= n_k:
                softmax((i - 1) % 2)
            if i >= 2:
                values(i - 2, i % 2)

    acc = acc_ref[...]
    out = acc[:ATT_HD] / acc[ATT_HD:ATT_HD + 1]
    o_ref[0] = jnp.concatenate([out[:, h * tq:(h + 1) * tq].T for h in range(ATT_GROUP)], axis=1).astype(BF16)


def _attn_call(q, k_c, v_c, k_x, v_x, n_lat):
    assert ATT_KV == 2
    nb, lq, _ = q.shape
    lc = k_c.shape[1]
    lx = k_x.shape[1]
    tq = _tile(lq, 512)
    tk = lc
    assert n_lat % tk == 0 and n_lat <= lx and tk % LANE == 0, (lc, lx, n_lat)
    n_k = 1 + n_lat // tk
    gw = ATT_GROUP * ATT_HD
    cols = ATT_GROUP * tq
    kern = functools.partial(_attn_kernel, tq=tq, tk=tk, n_k=n_k)
    kv = lambda n: pl.BlockSpec((1, n, KV_W), lambda b, g, i: (b, 0, 0))
    return pl.pallas_call(
        kern,
        out_shape=jax.ShapeDtypeStruct((nb, lq, Q_W), BF16),
        grid=(nb, ATT_KV, lq // tq),
        in_specs=[pl.BlockSpec((1, tq, gw), lambda b, g, i: (b, i, g)), kv(lc), kv(lc), kv(lx), kv(lx)],
        out_specs=pl.BlockSpec((1, tq, gw), lambda b, g, i: (b, i, g)),
        scratch_shapes=[pltpu.VMEM((cols, ATT_HD), BF16), pltpu.VMEM((n_k * tk, ATT_HD), BF16),
                        pltpu.VMEM((V_ROWS, n_k * tk), BF16), pltpu.VMEM((2, tk, cols), F32),
                        pltpu.VMEM((2, tk, cols), BF16), pltpu.VMEM((2, 1, cols), F32),
                        pltpu.VMEM((1, cols), F32), pltpu.VMEM((V_ROWS, cols), F32)],
        compiler_params=_params("parallel", "parallel", "arbitrary"),
        name="gqa_attn",
    )(q, k_c, v_c, k_x, v_x)


def _merge_kernel(pp_ref, pc_ref, pn_ref, x_ref, mod_ref, yd_ref, ya_ref, gate_ref, pw_ref, ps_ref,
                  wb_ref, wo_ref, o_ref, pad_ref, *, tm, seq):
    d = x_ref.shape[2]
    a0 = POOL_WIDTH
    a1 = POOL_WIDTH + DN_VW
    m = gate_ref[0, :, d:2 * d].astype(F32) * _dot(yd_ref[0], wb_ref[a0:a1, :])
    m = m + gate_ref[0, :, 2 * d:3 * d].astype(F32) * _dot(ya_ref[0], wb_ref[a1:, :])
    _fill_halo(pad_ref, pp_ref, pc_ref, pn_ref, tm)
    pooled = _window_means_minus_self(pad_ref, tm, seq)
    y_pool = (_dot(pooled.astype(BF16), pw_ref[...]) * ps_ref[...]).astype(BF16)
    m = m + gate_ref[0, :, 0:d].astype(F32) * _dot(y_pool, wb_ref[0:a0, :])
    o_ref[0] = x_ref[0] + mod_ref[0, 2] * _dot(m.astype(BF16), wo_ref[...])


def _merge_call(x, mod, sub, pool, y_dn, y_att, gate, pool_bd, pool_s, w_branch, w_out, layer):
    nb, t, d = x.shape
    tm = _tile(t, 512)
    row = lambda w: pl.BlockSpec((1, tm, w), lambda b, i: (b, i, 0))
    kern = functools.partial(_merge_kernel, tm=tm, seq=t)
    return pl.pallas_call(
        kern,
        out_shape=jax.ShapeDtypeStruct(x.shape, F32),
        grid=(nb, t // tm),
        in_specs=_halo_specs(tm, POOL_WIDTH, t) + [
            row(d), _mod_spec(mod, nb, sub, d), row(DN_VW), row(Q_W), row(N_BRANCH * d),
            _resident(pool_bd.shape), _resident(pool_s.shape),
            _layer_resident(w_branch, layer), _layer_resident(w_out, layer)],
        out_specs=row(d),
        scratch_shapes=[pltpu.VMEM((tm + 2 * SUBLANE, POOL_WIDTH), F32)],
        compiler_params=_params("parallel", "arbitrary"),
        name="merge",
    )(pool, pool, pool, x, mod, y_dn, y_att, gate, pool_bd, pool_s, w_branch, w_out)


def _block_diag_const(width, block, value):
    idx = np.arange(width)
    return jnp.asarray(np.where((idx[:, None] // block) == (idx[None, :] // block), value, 0.0), BF16)


def _pair_swap_const(width):
    idx = np.arange(width)
    return jnp.asarray((idx[:, None] == (idx[None, :] ^ 1)).astype(np.float32), BF16)


def _rope_tables(n_tokens):
    rows = n_tokens // GRID_W
    row = jnp.broadcast_to(jnp.arange(rows)[:, None], (rows, GRID_W)).reshape(-1)
    col = jnp.broadcast_to(jnp.arange(GRID_W)[None, :], (rows, GRID_W)).reshape(-1)
    n_freq = ATT_HD // 4
    inv = ROPE_BASE ** (-jnp.arange(n_freq, dtype=F32) / n_freq)
    ang = jnp.concatenate([row[:, None].astype(F32) * inv, col[:, None].astype(F32) * inv], axis=-1)
    cos = jnp.repeat(jnp.cos(ang), 2, axis=1)
    sin = jnp.repeat(jnp.sin(ang), 2, axis=1)
    sign = jnp.where(jnp.arange(ATT_HD) % 2 == 0, -1.0, 1.0).astype(F32)
    sin = sin * sign
    reps = LANE // ATT_HD
    return jnp.tile(cos, (1, reps)), jnp.tile(sin, (1, reps))


def _permute_w_in(w_in):
    o_pool = 0
    o_qkv = o_pool + POOL_WIDTH
    o_z = o_qkv + DN_CONV_CH
    o_ba = o_z + DN_VW
    o_aq = o_ba + 4 * DN_HEADS
    o_gate = o_aq + Q_W + 2 * KV_W
    pad = jnp.zeros(w_in.shape[:-1] + (BA_PAD - 4 * DN_HEADS,), w_in.dtype)
    parts = [w_in[..., o_pool:o_ba], w_in[..., o_aq:], w_in[..., o_ba:o_aq], pad]
    return jnp.concatenate(parts, axis=-1).astype(BF16)


def _head_select_const():
    sel = np.zeros((2, LANE, 2 * DN_VW), np.float32)
    for d in range(2):
        for h in range(DN_HEADS):
            sel[d, d * DN_HEADS + h, h * DN_DK:(h + 1) * DN_DK] = 1.0
            sel[d, 2 * DN_HEADS + d * DN_HEADS + h, DN_VW + h * DN_DK:DN_VW + (h + 1) * DN_DK] = 1.0
    return jnp.asarray(sel, BF16)


def _lane_vec(vals, offset):
    out = jnp.zeros((1, LANE), F32)
    return out.at[0, offset:offset + vals.size].set(vals.reshape(-1).astype(F32))


def kernel(x, c, ctx, c_ctx, w_ada, b_ada, norm_ffn1, w_ffn1_in, w_ffn1_out, norm_mix, w_in, pool_w, pool_scale, dn_conv, dn_a_log, dn_dt_bias, dn_norm, q_norm, k_norm, w_branch, w_out, norm_ffn2, w_ffn2_in, w_ffn2_out, norm_final):
    nb, seq, d = x.shape
    lc = ctx.shape[1]
    depth = w_ada.shape[0]

    n_rows = -(-(nb + 1) // SUBLANE) * SUBLANE
    cond = jnp.concatenate([c, c_ctx[None], jnp.zeros((n_rows - nb - 1, d), F32)], axis=0)
    mod_all = _ada_call(cond, w_ada, b_ada).reshape(depth, n_rows, N_MOD, 1, d)

    cos_x, sin_x = _rope_tables(seq)
    cos_c = jnp.ones((lc, LANE), F32)
    sin_c = jnp.zeros((lc, LANE), F32)
    e_q = _block_diag_const(Q_W, ATT_HD, 1.0 / ATT_HD)
    e_k = _block_diag_const(KV_W, ATT_HD, 1.0 / ATT_HD)
    p_q = _pair_swap_const(Q_W)
    p_k = _pair_swap_const(KV_W)
    e_sum = _block_diag_const(DN_QK, DN_DK, 1.0)
    e_mean = _block_diag_const(DN_VW, DN_DV, 1.0 / DN_DV)
    head_sel = _head_select_const()

    wf1_in = w_ffn1_in.astype(BF16)
    wf1_out = w_ffn1_out.astype(BF16)
    wf2_in = w_ffn2_in.astype(BF16)
    wf2_out = w_ffn2_out.astype(BF16)
    w_perm = _permute_w_in(w_in)
    w_br = w_branch.astype(BF16)
    w_o = w_out.astype(BF16)

    xc = ctx.reshape(1, nb * lc, d)
    for l in range(depth):
        ctx_out = l < depth - 1
        mod_x = mod_all[l, :nb]
        mod_c = mod_all[l, nb:nb + 1]
        qg = jnp.tile(q_norm[l], ATT_HEADS).reshape(1, Q_W)
        kg = jnp.tile(k_norm[l], ATT_KV).reshape(1, KV_W)
        pool_bd = jax.scipy.linalg.block_diag(*[pool_w[l, g] for g in range(len(POOL_WINDOWS))]).astype(BF16)
        pool_s = pool_scale[l].reshape(1, POOL_WIDTH)
        alog = _lane_vec(dn_a_log[l], 2 * DN_HEADS)
        dtb = _lane_vec(dn_dt_bias[l], 2 * DN_HEADS)
        gn = jnp.tile(dn_norm[l], DN_HEADS).reshape(1, DN_VW)

        x = _ffn_call(x, mod_x, 0, norm_ffn1[l], wf1_in, wf1_out, l, norm_final)
        xc = _ffn_call(xc, mod_c, 0, norm_ffn1[l], wf1_in, wf1_out, l, norm_final)

        proj_consts = (qg, kg, e_q, p_q, e_k, p_k, dn_conv[l], e_sum)
        xc_seq = xc.reshape(nb, lc, d)
        pool_x, qd_x, kd_x, vd_x, z_x, ba_x, aq_x, ak_x, av_x, gate_x = _proj_call(
            x, mod_x, 1, norm_mix[l], w_perm, l, cos_x, sin_x, *proj_consts)
        pool_c, qd_c, kd_c, vd_c, z_c, ba_c, aq_c, ak_c, av_c, gate_c = _proj_call(
            xc_seq, mod_c, 1, norm_mix[l], w_perm, l, cos_c, sin_c, *proj_consts)

        yd_c, yd_x = _delta_call((qd_c, kd_c, vd_c, ba_c, z_c), (qd_x, kd_x, vd_x, ba_x, z_x),
                                 alog, dtb, gn, e_mean, head_sel)

        ya_x = _attn_call(aq_x, ak_c, av_c, ak_x, av_x, seq)
        x = _merge_call(x, mod_x, 1, pool_x, yd_x, ya_x, gate_x, pool_bd, pool_s, w_br, w_o, l)

        if ctx_out:
            ya_c = _attn_call(aq_c, ak_c, av_c, ak_c, av_c, 0)
            xc = _merge_call(xc_seq, mod_c, 1, pool_c, yd_c, ya_c, gate_c, pool_bd, pool_s, w_br, w_o, l)
            xc = xc.reshape(1, nb * lc, d)

        x = _ffn_call(x, mod_x, 2, norm_ffn2[l], wf2_in, wf2_out, l, norm_final, out_norm=not ctx_out)
        if ctx_out:
            xc = _ffn_call(xc, mod_c, 2, norm_ffn2[l], wf2_in, wf2_out, l, norm_final)

    return x
```

```python
import functools

import jax
import jax.numpy as jnp
import numpy as np
from jax import lax
from jax.experimental import pallas as pl
from jax.experimental.pallas import tpu as pltpu

F32 = jnp.float32
BF16 = jnp.bfloat16

N_MOD = 9
FFN_RES = 0.5
POOL_WIDTH = 256
POOL_WINDOWS = (2, 4, 8, 16)
POOL_GROUP = 64
DN_HEADS = 4
DN_DK = 64
DN_DV = 64
DN_CONV = 5
DN_CHUNK = 64
DN_QK = DN_HEADS * DN_DK
DN_VW = DN_HEADS * DN_DV
DN_CONV_CH = 2 * DN_QK + DN_VW
ATT_HEADS = 8
ATT_KV = 2
ATT_GROUP = ATT_HEADS // ATT_KV
ATT_HD = 64
ROPE_BASE = 10000.0
GRID_W = 64
N_BRANCH = 3
EPS = 1e-6
LOG2E = 1.4426950408889634

LANE = 128
SUBLANE = 8
VMEM_LIMIT = 56 * 1024 * 1024

BA_PAD = LANE
Q_W = ATT_HEADS * ATT_HD
KV_W = ATT_KV * ATT_HD
V_ROWS = ATT_HD + 16
C_POOL = 0
C_QKV = C_POOL + POOL_WIDTH
C_Z = C_QKV + DN_CONV_CH
C_AQ = C_Z + DN_VW
C_AK = C_AQ + Q_W
C_AV = C_AK + KV_W
C_GATE = C_AV + KV_W


def _params(*sem):
    return pltpu.CompilerParams(dimension_semantics=sem, vmem_limit_bytes=VMEM_LIMIT)


def _tile(n, target):
    t = min(target, n)
    assert n % t == 0 and t % SUBLANE == 0, (n, t)
    return t


def _resident(shape):
    nd = len(shape)
    return pl.BlockSpec(shape, lambda *_: (0,) * nd, pipeline_mode=pl.Buffered(1))


def _layer_resident(stacked, layer):
    nd = stacked.ndim - 1
    return pl.BlockSpec((None,) + stacked.shape[1:], lambda *_: (layer,) + (0,) * nd,
                        pipeline_mode=pl.Buffered(1))


def _silu(x):
    return x * jax.nn.sigmoid(x)


def _dot(a, b):
    return jnp.dot(a, b, preferred_element_type=F32)


def _dot_nt(a, b):
    return lax.dot_general(a, b, (((1,), (1,)), ((), ())), preferred_element_type=F32)


def _dot_tn(a, b):
    return lax.dot_general(a, b, (((0,), (0,)), ((), ())), preferred_element_type=F32)


def _split3(x):
    hi = x.astype(BF16)
    r1 = x - hi.astype(F32)
    mid = r1.astype(BF16)
    lo = (r1 - mid.astype(F32)).astype(BF16)
    return hi, mid, lo


def _dot_exact_lhs(a_bf16, x_f32):
    hi, mid, lo = _split3(x_f32)
    return _dot(a_bf16, hi) + _dot(a_bf16, mid) + _dot(a_bf16, lo)


def _dot_exact_rhs(x_f32, b_bf16):
    hi, mid, lo = _split3(x_f32)
    return _dot(hi, b_bf16) + _dot(mid, b_bf16) + _dot(lo, b_bf16)


def _ada_kernel(s_ref, w_ref, b_ref, o_ref):
    s = _silu(s_ref[...])
    w = w_ref[0]
    s_hi, s_mid, s_lo = _split3(s)
    w_hi, w_mid, w_lo = _split3(w)
    acc = _dot(s_hi, w_hi)
    acc = acc + _dot(s_hi, w_mid) + _dot(s_mid, w_hi)
    acc = acc + _dot(s_hi, w_lo) + _dot(s_mid, w_mid) + _dot(s_lo, w_hi)
    o_ref[0] = acc + b_ref[0]


def _ada_call(cond, w_ada, b_ada):
    depth, d, n = w_ada.shape
    rows = cond.shape[0]
    tn = 1024
    return pl.pallas_call(
        _ada_kernel,
        out_shape=jax.ShapeDtypeStruct((depth, rows, n), F32),
        grid=(depth, n // tn),
        in_specs=[
            pl.BlockSpec((rows, d), lambda l, j: (0, 0)),
            pl.BlockSpec((1, d, tn), lambda l, j: (l, 0, j)),
            pl.BlockSpec((1, 1, tn), lambda l, j: (l, 0, j)),
        ],
        out_specs=pl.BlockSpec((1, rows, tn), lambda l, j: (l, 0, j)),
        compiler_params=_params("parallel", "parallel"),
        name="ada_mod",
    )(cond, w_ada, b_ada.reshape(depth, 1, n))


def _norm_mod(x, gain, shift, scale):
    ms = jnp.mean(x * x, axis=-1, keepdims=True)
    h = x * lax.rsqrt(ms + EPS) * gain
    return h * (1.0 + scale) + shift


def _ffn_kernel(x_ref, mod_ref, g_ref, wi_ref, wo_ref, og_ref, o_ref, acc_ref, *, d_ff, tf, out_norm):
    x = x_ref[0]
    h = _norm_mod(x, g_ref[...], mod_ref[0, 0], mod_ref[0, 1]).astype(BF16)
    for j in range(d_ff // tf):
        g = _dot(h, wi_ref[:, j * tf:(j + 1) * tf])
        u = _dot(h, wi_ref[:, d_ff + j * tf:d_ff + (j + 1) * tf])
        a = (_silu(g) * u).astype(BF16)
        part = _dot(a, wo_ref[j * tf:(j + 1) * tf, :])
        if j == 0:
            acc_ref[...] = part
        else:
            acc_ref[...] += part
    y = x + (FFN_RES * mod_ref[0, 2]) * acc_ref[...]
    if out_norm:
        y = y * lax.rsqrt(jnp.mean(y * y, axis=-1, keepdims=True) + EPS) * og_ref[...]
    o_ref[0] = y


def _ffn_call(x, mod, sub, gain, w_in, w_out, layer, out_gain, out_norm=False):
    nb, t, d = x.shape
    d_ff = w_out.shape[1]
    tm = _tile(t, 512)
    tf = 256
    kern = functools.partial(_ffn_kernel, d_ff=d_ff, tf=tf, out_norm=out_norm)
    return pl.pallas_call(
        kern,
        out_shape=jax.ShapeDtypeStruct(x.shape, F32),
        grid=(nb, t // tm),
        in_specs=[
            pl.BlockSpec((1, tm, d), lambda b, i: (b, i, 0)),
            pl.BlockSpec((1, 3, 1, d), lambda b, i: (b, sub, 0, 0)),
            _resident((1, d)),
            _layer_resident(w_in, layer),
            _layer_resident(w_out, layer),
            _resident((1, d)),
        ],
        out_specs=pl.BlockSpec((1, tm, d), lambda b, i: (b, i, 0)),
        scratch_shapes=[pltpu.VMEM((tm, d), F32)],
        compiler_params=_params("parallel", "parallel"),
        name="ffn",
    )(x, mod, gain.reshape(1, d), w_in, w_out, out_gain.reshape(1, d))


def _head_norm_rope(t_raw, gain, e_mat, p_mat, cos, sin, out_scale):
    ms = _dot((t_raw * t_raw).astype(BF16), e_mat)
    r = lax.rsqrt(ms + EPS) * out_scale
    t = t_raw * gain
    t_sw = _dot(t.astype(BF16), p_mat)
    w = t_raw.shape[1]
    outs = []
    for g in range(w // LANE):
        sl = slice(g * LANE, (g + 1) * LANE)
        outs.append(r[:, sl] * (t[:, sl] * cos + t_sw[:, sl] * sin))
    return jnp.concatenate(outs, axis=1) if len(outs) > 1 else outs[0]


def _proj_kernel(xp_ref, x_ref, xn_ref, mod_ref, g_ref, w_ref, cos_ref, sin_ref, qg_ref, kg_ref,
                 eq_ref, pq_ref, ek_ref, pk_ref, cw_ref, es_ref,
                 pool_ref, dq_ref, dk_ref, dv_ref, z_ref, ba_ref, q_ref, k_ref, v_ref, gate_ref,
                 pad_ref, *, n_gate, tm):
    gain = g_ref[...]
    shift = mod_ref[0, 0]
    scale = mod_ref[0, 1]
    h = _norm_mod(x_ref[0], gain, shift, scale).astype(BF16)

    def cols(lo, width):
        return _dot(h, w_ref[:, lo:lo + width])

    pool_ref[0] = cols(C_POOL, POOL_WIDTH)

    i = pl.program_id(1)
    n = pl.num_programs(1)
    halo = jnp.concatenate([xp_ref[0], xn_ref[0]], axis=0)
    h_halo = _norm_mod(halo, gain, shift, scale).astype(BF16)
    w_qkv = w_ref[:, C_QKV:C_QKV + DN_CONV_CH]
    qkv_halo = _dot(h_halo, w_qkv)
    pad_ref[0:SUBLANE, :] = jnp.where(i > 0, qkv_halo[0:SUBLANE], 0.0)
    pad_ref[SUBLANE + tm:2 * SUBLANE + tm, :] = jnp.where(i < n - 1, qkv_halo[SUBLANE:2 * SUBLANE], 0.0)
    for j in range(DN_CONV_CH // 256):
        pad_ref[SUBLANE:SUBLANE + tm, j * 256:(j + 1) * 256] = cols(C_QKV + j * 256, 256)
    z_ref[0] = cols(C_Z, DN_VW).astype(BF16)
    q_raw = cols(C_AQ, Q_W)
    k_raw = cols(C_AK, KV_W)
    v_ref[0] = cols(C_AV, KV_W).astype(BF16)
    tg = 512
    for j in range(n_gate // tg):
        gate_ref[0, :, j * tg:(j + 1) * tg] = jax.nn.sigmoid(cols(C_GATE + j * tg, tg)).astype(BF16)
    ba_ref[0] = cols(C_GATE + n_gate, BA_PAD)

    cos = cos_ref[...]
    sin = sin_ref[...]
    q = _head_norm_rope(q_raw, qg_ref[...], eq_ref[...], pq_ref[...], cos, sin, (ATT_HD ** -0.5) * LOG2E)
    q_ref[0] = q.astype(BF16)
    k = _head_norm_rope(k_raw, kg_ref[...], ek_ref[...], pk_ref[...], cos, sin, 1.0)
    k_ref[0] = k.astype(BF16)

    half = DN_CONV // 2
    n_pad = tm + 2 * SUBLANE
    for part, o_ref in enumerate((dq_ref, dk_ref, dv_ref)):
        cs = slice(part * DN_QK, (part + 1) * DN_QK)
        xp = pad_ref[:, cs]
        acc = None
        for j in range(DN_CONV):
            shifted = xp if j == half else pltpu.roll(xp, (half - j) % n_pad, 0)
            term = shifted[SUBLANE:SUBLANE + tm] * cw_ref[j:j + 1, cs]
            acc = term if acc is None else acc + term
        y = _silu(acc)
        if part < 2:
            y = y * lax.rsqrt(_dot((y * y).astype(BF16), es_ref[...]) + EPS)
            if part == 0:
                y = y * (DN_DK ** -0.5)
        o_ref[0] = y.astype(BF16)


def _mod_spec(mod, nb, sub, d):
    if mod.shape[0] == nb:
        return pl.BlockSpec((1, 3, 1, d), lambda b, i: (b, sub, 0, 0))
    return pl.BlockSpec((1, 3, 1, d), lambda b, i: (0, sub, 0, 0))


def _proj_call(x, mod, sub, gain, w_perm, layer, cos, sin, qg, kg, eq, pq, ek, pk, conv_w, e_sum):
    nb, t, d = x.shape
    n_gate = N_BRANCH * d
    tm = _tile(t, 512)
    nt = t // tm
    row = lambda w: pl.BlockSpec((1, tm, w), lambda b, i: (b, i, 0))
    widths = [(POOL_WIDTH, F32), (DN_QK, BF16), (DN_QK, BF16), (DN_VW, BF16), (DN_VW, BF16), (BA_PAD, F32),
              (Q_W, BF16), (KV_W, BF16), (KV_W, BF16), (n_gate, BF16)]
    kern = functools.partial(_proj_kernel, n_gate=n_gate, tm=tm)
    return pl.pallas_call(
        kern,
        out_shape=[jax.ShapeDtypeStruct((nb, t, wd), dt) for wd, dt in widths],
        grid=(nb, nt),
        in_specs=_halo_specs(tm, d, t) + [
            _mod_spec(mod, nb, sub, d),
            _resident((1, d)),
            _layer_resident(w_perm, layer),
            pl.BlockSpec((tm, LANE), lambda b, i: (i, 0)),
            pl.BlockSpec((tm, LANE), lambda b, i: (i, 0)),
            _resident((1, Q_W)),
            _resident((1, KV_W)),
            _resident((Q_W, Q_W)),
            _resident((Q_W, Q_W)),
            _resident((KV_W, KV_W)),
            _resident((KV_W, KV_W)),
            _resident(conv_w.shape),
            _resident(e_sum.shape),
        ],
        out_specs=[row(wd) for wd, _ in widths],
        scratch_shapes=[pltpu.VMEM((tm + 2 * SUBLANE, DN_CONV_CH), F32)],
        compiler_params=_params("parallel", "arbitrary"),
        name="mix_proj",
    )(x, x, x, mod, gain.reshape(1, d), w_perm, cos, sin, qg, kg, eq, pq, ek, pk, conv_w, e_sum)


def _fill_halo(pad_ref, prev_ref, cur_ref, next_ref, tt):
    i = pl.program_id(1)
    n = pl.num_programs(1)
    prev = prev_ref[0]
    nxt = next_ref[0]
    pad_ref[0:SUBLANE, :] = jnp.where(i > 0, prev, jnp.zeros_like(prev))
    pad_ref[SUBLANE:SUBLANE + tt, :] = cur_ref[0]
    pad_ref[SUBLANE + tt:2 * SUBLANE + tt, :] = jnp.where(i < n - 1, nxt, jnp.zeros_like(nxt))


def _halo_specs(tt, width, n_rows):
    per = tt // SUBLANE
    last = n_rows // SUBLANE - 1
    return [
        pl.BlockSpec((1, SUBLANE, width), lambda b, i: (b, jnp.maximum(i * per - 1, 0), 0)),
        pl.BlockSpec((1, tt, width), lambda b, i: (b, i, 0)),
        pl.BlockSpec((1, SUBLANE, width), lambda b, i: (b, jnp.minimum((i + 1) * per, last), 0)),
    ]


def _window_means_minus_self(pad_ref, tt, seq):
    assert POOL_WINDOWS == (2, 4, 8, 16) and POOL_GROUP * 2 == LANE
    i = pl.program_id(1)
    t = i * tt + lax.broadcasted_iota(jnp.int32, (tt, 1), 0)
    lane = lax.broadcasted_iota(jnp.int32, (1, LANE), 1)
    n = tt + 2 * SUBLANE
    body = slice(SUBLANE, SUBLANE + tt)

    def count(w):
        return (jnp.minimum(t + w - w // 2, seq) - jnp.maximum(t - w // 2, 0)).astype(F32)

    def ahead(a, k):
        return pltpu.roll(a, n - k, 0)

    def behind(a, k):
        return pltpu.roll(a, k, 0)

    tiles = []
    for lt in range(POOL_WIDTH // LANE):
        x = pad_ref[:, lt * LANE:(lt + 1) * LANE]
        r2 = x + ahead(x, 1)
        r4 = r2 + ahead(r2, 2)
        if lt == 0:
            narrow = behind(r2, 1)[body] / count(2)
            wide = behind(r4, 2)[body] / count(4)
        else:
            r8 = r4 + ahead(r4, 4)
            narrow = behind(r8, 4)[body] / count(8)
            wide = (r8[0:tt] + r8[body]) / count(16)
        tiles.append(jnp.where(lane >= POOL_GROUP, wide, narrow) - x[body])
    return jnp.concatenate(tiles, axis=1)


def _delta_kernel(qc_ref, kc_ref, vc_ref, bac_ref, zc_ref, qx_ref, kx_ref, vx_ref, bax_ref, zx_ref,
                  alog_ref, dtb_ref, gn_ref, e_ref, sel_ref, yc_ref, yx_ref,
                  sf_ref, sb_ref, ocf_ref, ocb_ref, oxf_ref, oxb_ref, *, n_ctx, n_lat, group):
    c = DN_CHUNK
    w = DN_HEADS * c
    row = lax.broadcasted_iota(jnp.int32, (w, w), 0)
    col = lax.broadcasted_iota(jnp.int32, (w, w), 1)
    rc_xor = row ^ col
    same = (rc_xor >> 6) == 0
    rp = row & (c - 1)
    cp = col & (c - 1)
    eye = jnp.where(row == col, 1.0, 0.0).astype(F32)
    m_incl = (same & (rp >= cp), same & (rp <= cp))
    m_strict = (same & (rp > cp), same & (rp < cp))
    r64 = lax.broadcasted_iota(jnp.int32, (c, c), 0)
    c64 = lax.broadcasted_iota(jnp.int32, (c, c), 1)
    cum_mat = (jnp.where(r64 >= c64, 1.0, 0.0).astype(BF16), jnp.where(r64 <= c64, 1.0, 0.0).astype(BF16))
    neg_a = -jnp.exp(alog_ref[...])
    dtb = dtb_ref[...]

    sf_ref[...] = jnp.zeros_like(sf_ref)
    sb_ref[...] = jnp.zeros_like(sb_ref)

    def spread(a):
        return jnp.where(same, jnp.concatenate([a, a, a, a], axis=0), 0.0)

    def gather_heads(o):
        return o[0:c] + o[c:2 * c] + o[2 * c:3 * c] + o[3 * c:4 * c]

    def prepare(refs, chunks):
        q_ref, k_ref, v_ref, ba_ref = refs
        st = []
        for rows, d in chunks:
            q = q_ref[0, rows, :].astype(F32)
            k = k_ref[0, rows, :].astype(F32)
            v = v_ref[0, rows, :].astype(F32)
            ba = ba_ref[0, rows, :]
            g_all = neg_a * jax.nn.softplus(ba + dtb)
            cum = _dot_exact_lhs(cum_mat[d], g_all)
            st.append(dict(d=d, q=q, k=k, v=v, ba=ba, cum=cum))
        for x in st:
            sel = sel_ref[x["d"]]
            x["beta"] = _dot(jax.nn.sigmoid(x["ba"]).astype(BF16), sel[:, :w])
            x["g_cum"] = _dot_exact_rhs(x["cum"], sel[:, w:])
        for x in st:
            x["kb"] = x["k"] * x["beta"]
            k_bd = spread(x["k"]).astype(BF16)
            x["prod"] = _dot_nt(jnp.concatenate([spread(x["kb"]), spread(x["q"])], axis=0).astype(BF16), k_bd)
        for x in st:
            d = x["d"]
            g_col = spread(x["g_cum"])
            decay = jnp.where(m_incl[d], jnp.exp(g_col - g_col.T), 0.0)
            a = jnp.where(m_strict[d], x["prod"][:w] * decay, 0.0)
            x["qkd"] = (x["prod"][w:] * decay).astype(BF16)
            x["a"] = a
            a4 = jnp.where((rc_xor >> 2) == 0, a, 0.0)
            x["t0"] = eye - a4
            x["a4"] = a4.astype(BF16)
        for x in st:
            x["a4sq"] = _dot(x["a4"], x["a4"]).astype(BF16)
        for x in st:
            x["t"] = x["t0"] + _dot(x["t0"].astype(BF16), x["a4sq"])
        for lg in range(2, 6):
            for x in st:
                b_ring = jnp.where((rc_xor >> lg) == 1, x["a"], 0.0).astype(BF16)
                x["t_b"] = x["t"].astype(BF16)
                x["tb"] = _dot(x["t_b"], b_ring).astype(BF16)
            for x in st:
                x["t"] = x["t"] - _dot(x["tb"], x["t_b"])
        out = []
        for x in st:
            last = c - 1 if x["d"] == 0 else 0
            g_cum = x["g_cum"]
            g_tot = g_cum[last:last + 1, :]
            e_g = jnp.exp(g_cum)
            rhs = jnp.concatenate([spread(x["v"] * x["beta"]), spread(x["kb"] * e_g)], axis=1).astype(BF16)
            uw = _dot(x["t"].astype(BF16), rhs)
            out.append(dict(u=uw[:, :w],
                            lhs=jnp.concatenate([uw[:, w:], spread(x["q"] * e_g)], axis=0).astype(BF16),
                            qkd=x["qkd"], kd=spread(x["k"] * jnp.exp(g_tot - g_cum)).astype(BF16),
                            e_tot=jnp.exp(g_tot)))
        return out

    def advance(pre, s_ref):
        s_old = s_ref[...]
        ws_qs = _dot(pre["lhs"], s_old.astype(BF16))
        v_new = (pre["u"] - ws_qs[:w]).astype(BF16)
        o = ws_qs[w:] + _dot(pre["qkd"], v_new)
        s_ref[...] = s_old * pre["e_tot"] + _dot_tn(pre["kd"], v_new)
        return gather_heads(o)

    def run_block(refs, of_ref, ob_ref, rows_f, rows_b):
        chunks = []
        for g in range(len(rows_f)):
            chunks += [(rows_f[g], 0), (rows_b[g], 1)]
        pre = prepare(refs, chunks)
        for g in range(len(rows_f)):
            of_ref[rows_f[g], :] = advance(pre[2 * g], sf_ref)
            ob_ref[rows_b[g], :] = advance(pre[2 * g + 1], sb_ref)

    ctx_refs = (qc_ref, kc_ref, vc_ref, bac_ref)
    lat_refs = (qx_ref, kx_ref, vx_ref, bax_ref)
    for j in range(n_ctx // group):
        run_block(ctx_refs, ocf_ref, ocb_ref,
                  [pl.ds((j * group + g) * c, c) for g in range(group)],
                  [pl.ds((n_ctx - 1 - j * group - g) * c, c) for g in range(group)])

    def lat_body(j, carry):
        rows_f = [pl.ds(pl.multiple_of((j * group + g) * c, c), c) for g in range(group)]
        rows_b = [pl.ds(pl.multiple_of((n_lat - 1 - j * group - g) * c, c), c) for g in range(group)]
        run_block(lat_refs, oxf_ref, oxb_ref, rows_f, rows_b)
        return carry

    lax.fori_loop(0, n_lat // group, lat_body, 0)

    def finish(of_ref, ob_ref, z_ref, y_ref, n_rows):
        tt = min(512, n_rows)
        for j in range(n_rows // tt):
            rs = slice(j * tt, (j + 1) * tt)
            o = of_ref[rs, :] + ob_ref[rs, :]
            ms = _dot((o * o).astype(BF16), e_ref[...])
            y = o * lax.rsqrt(ms + EPS) * gn_ref[...]
            y_ref[0, rs, :] = (y * _silu(z_ref[0, rs, :].astype(F32))).astype(BF16)

    finish(ocf_ref, ocb_ref, zc_ref, yc_ref, n_ctx * c)
    finish(oxf_ref, oxb_ref, zx_ref, yx_ref, n_lat * c)


def _delta_call(ctx_parts, lat_parts, alog, dtb, gn, e_mean, sel):
    qc, kc, vc, bac, zc = ctx_parts
    qx, kx, vx, bax, zx = lat_parts
    nb, lc, wd = qc.shape
    lx = qx.shape[1]
    group = 2
    assert lc % (group * DN_CHUNK) == 0 and lx % (group * DN_CHUNK) == 0, (lc, lx)
    assert lc % _tile(lc, 512) == 0 and lx % _tile(lx, 512) == 0
    kern = functools.partial(_delta_kernel, n_ctx=lc // DN_CHUNK, n_lat=lx // DN_CHUNK, group=group)
    seq = lambda n, width: pl.BlockSpec((1, n, width), lambda b: (b, 0, 0))
    in_specs = [seq(lc, wd), seq(lc, wd), seq(lc, wd), seq(lc, BA_PAD), seq(lc, wd),
                seq(lx, wd), seq(lx, wd), seq(lx, wd), seq(lx, BA_PAD), seq(lx, wd),
                _resident((1, LANE)), _resident((1, LANE)), _resident((1, wd)), _resident((wd, wd)),
                _resident(sel.shape)]
    return pl.pallas_call(
        kern,
        out_shape=[jax.ShapeDtypeStruct((nb, lc, wd), BF16), jax.ShapeDtypeStruct((nb, lx, wd), BF16)],
        grid=(nb,),
        in_specs=in_specs,
        out_specs=[seq(lc, wd), seq(lx, wd)],
        scratch_shapes=[pltpu.VMEM((wd, wd), F32), pltpu.VMEM((wd, wd), F32),
                        pltpu.VMEM((lc, wd), F32), pltpu.VMEM((lc, wd), F32),
                        pltpu.VMEM((lx, wd), F32), pltpu.VMEM((lx, wd), F32)],
        compiler_params=_params("parallel"),
        name="delta_rule",
    )(qc, kc, vc, bac, zc, qx, kx, vx, bax, zx, alog, dtb, gn, e_mean, sel)


P_CLAMP = 40.0
P_OVERFLOW = 2.0 ** 39


def _attn_kernel(q_ref, kc_ref, vc_ref, kx_ref, vx_ref, o_ref, qs_ref, kh_ref, vt_ref, s_ref, p_ref, al_ref, m_ref,
                 acc_ref, *, tq, tk, n_k):
    for h in range(ATT_GROUP):
        qs_ref[h * tq:(h + 1) * tq, :] = q_ref[0, :, h * ATT_HD:(h + 1) * ATT_HD]

    @pl.when(pl.program_id(2) == 0)
    def _():
        head = pl.program_id(1)
        pieces = [(kc_ref, vc_ref, 0, tk)] + ([(kx_ref, vx_ref, tk, (n_k - 1) * tk)] if n_k > 1 else [])
        for k_src, v_src, lo, n in pieces:
            k_all = k_src[0, 0:n, :]
            kh_ref[lo:lo + n, :] = jnp.where(head == 0, k_all[:, 0:ATT_HD], k_all[:, ATT_HD:2 * ATT_HD])
            v_t = v_src[0, 0:n, :].astype(F32).T
            vt_ref[0:ATT_HD, lo:lo + n] = jnp.where(head == 0, v_t[0:ATT_HD], v_t[ATT_HD:2 * ATT_HD]).astype(BF16)
        tail = lax.broadcasted_iota(jnp.int32, (V_ROWS - ATT_HD, n_k * tk), 0) == 0
        vt_ref[ATT_HD:V_ROWS, :] = jnp.where(tail, 1.0, 0.0).astype(BF16)

    def keys(j):
        return kh_ref[j * tk:(j + 1) * tk, :]

    def vals(j):
        return vt_ref[:, j * tk:(j + 1) * tk]

    s0 = _dot_nt(keys(0), qs_ref[...])
    m0 = jnp.max(s0, axis=0, keepdims=True)
    m_ref[...] = m0
    acc_ref[...] = _dot(vals(0), jnp.exp2(s0 - m0).astype(BF16))

    def probs(j, slot):
        s = _dot_nt(keys(j), qs_ref[...])
        p_ref[slot] = jnp.exp2(jnp.minimum(s - m_ref[...], P_CLAMP)).astype(BF16)

    def accumulate(j, slot):
        acc_ref[...] += _dot(vals(j), p_ref[slot])

    for i in range(1, n_k + 1):
        if i < n_k:
            probs(i, i % 2)
        if i >= 2:
            accumulate(i - 1, (i - 1) % 2)

    @pl.when(jnp.max(acc_ref[ATT_HD:ATT_HD + 1, :]) >= P_OVERFLOW)
    def _():
        m_ref[...] = jnp.full(m_ref.shape, -1e30, F32)
        acc_ref[...] = jnp.zeros(acc_ref.shape, F32)

        def scores(j, slot):
            s_ref[slot] = _dot_nt(keys(j), qs_ref[...])

        def softmax(slot):
            s = s_ref[slot]
            m_old = m_ref[...]
            m_new = jnp.maximum(m_old, jnp.max(s, axis=0, keepdims=True))
            p_ref[slot] = jnp.exp2(s - m_new).astype(BF16)
            al_ref[slot] = jnp.exp2(m_old - m_new)
            m_ref[...] = m_new

        def values(j, slot):
            acc_ref[...] = acc_ref[...] * al_ref[slot] + _dot(vals(j), p_ref[slot])

        for i in range(n_k + 2):
            if i < n_k:
                scores(i, i % 2)
            if 1 <= i <= n_k:
                softmax((i - 1) % 2)
            if i >= 2:
                values(i - 2, i % 2)

    acc = acc_ref[...]
    out = acc[:ATT_HD] / acc[ATT_HD:ATT_HD + 1]
    o_ref[0] = jnp.concatenate([out[:, h * tq:(h + 1) * tq].T for h in range(ATT_GROUP)], axis=1).astype(BF16)


def _attn_call(q, k_c, v_c, k_x, v_x, n_lat):
    assert ATT_KV == 2
    nb, lq, _ = q.shape
    lc = k_c.shape[1]
    lx = k_x.shape[1]
    tq = _tile(lq, 512)
    tk = lc
    assert n_lat % tk == 0 and n_lat <= lx and tk % LANE == 0, (lc, lx, n_lat)
    n_k = 1 + n_lat // tk
    gw = ATT_GROUP * ATT_HD
    cols = ATT_GROUP * tq
    kern = functools.partial(_attn_kernel, tq=tq, tk=tk, n_k=n_k)
    kv = lambda n: pl.BlockSpec((1, n, KV_W), lambda b, g, i: (b, 0, 0))
    return pl.pallas_call(
        kern,
        out_shape=jax.ShapeDtypeStruct((nb, lq, Q_W), BF16),
        grid=(nb, ATT_KV, lq // tq),
        in_specs=[pl.BlockSpec((1, tq, gw), lambda b, g, i: (b, i, g)), kv(lc), kv(lc), kv(lx), kv(lx)],
        out_specs=pl.BlockSpec((1, tq, gw), lambda b, g, i: (b, i, g)),
        scratch_shapes=[pltpu.VMEM((cols, ATT_HD), BF16), pltpu.VMEM((n_k * tk, ATT_HD), BF16),
                        pltpu.VMEM((V_ROWS, n_k * tk), BF16), pltpu.VMEM((2, tk, cols), F32),
                        pltpu.VMEM((2, tk, cols), BF16), pltpu.VMEM((2, 1, cols), F32),
                        pltpu.VMEM((1, cols), F32), pltpu.VMEM((V_ROWS, cols), F32)],
        compiler_params=_params("parallel", "parallel", "arbitrary"),
        name="gqa_attn",
    )(q, k_c, v_c, k_x, v_x)


def _merge_kernel(pp_ref, pc_ref, pn_ref, x_ref, mod_ref, yd_ref, ya_ref, gate_ref, pw_ref, ps_ref,
                  wb_ref, wo_ref, o_ref, pad_ref, *, tm, seq):
    d = x_ref.shape[2]
    a0 = POOL_WIDTH
    a1 = POOL_WIDTH + DN_VW
    m = gate_ref[0, :, d:2 * d].astype(F32) * _dot(yd_ref[0], wb_ref[a0:a1, :])
    m = m + gate_ref[0, :, 2 * d:3 * d].astype(F32) * _dot(ya_ref[0], wb_ref[a1:, :])
    _fill_halo(pad_ref, pp_ref, pc_ref, pn_ref, tm)
    pooled = _window_means_minus_self(pad_ref, tm, seq)
    y_pool = (_dot(pooled.astype(BF16), pw_ref[...]) * ps_ref[...]).astype(BF16)
    m = m + gate_ref[0, :, 0:d].astype(F32) * _dot(y_pool, wb_ref[0:a0, :])
    o_ref[0] = x_ref[0] + mod_ref[0, 2] * _dot(m.astype(BF16), wo_ref[...])


def _merge_call(x, mod, sub, pool, y_dn, y_att, gate, pool_bd, pool_s, w_branch, w_out, layer):
    nb, t, d = x.shape
    tm = _tile(t, 512)
    row = lambda w: pl.BlockSpec((1, tm, w), lambda b, i: (b, i, 0))
    kern = functools.partial(_merge_kernel, tm=tm, seq=t)
    return pl.pallas_call(
        kern,
        out_shape=jax.ShapeDtypeStruct(x.shape, F32),
        grid=(nb, t // tm),
        in_specs=_halo_specs(tm, POOL_WIDTH, t) + [
            row(d), _mod_spec(mod, nb, sub, d), row(DN_VW), row(Q_W), row(N_BRANCH * d),
            _resident(pool_bd.shape), _resident(pool_s.shape),
            _layer_resident(w_branch, layer), _layer_resident(w_out, layer)],
        out_specs=row(d),
        scratch_shapes=[pltpu.VMEM((tm + 2 * SUBLANE, POOL_WIDTH), F32)],
        compiler_params=_params("parallel", "arbitrary"),
        name="merge",
    )(pool, pool, pool, x, mod, y_dn, y_att, gate, pool_bd, pool_s, w_branch, w_out)


def _block_diag_const(width, block, value):
    idx = np.arange(width)
    return jnp.asarray(np.where((idx[:, None] // block) == (idx[None, :] // block), value, 0.0), BF16)


def _pair_swap_const(width):
    idx = np.arange(width)
    return jnp.asarray((idx[:, None] == (idx[None, :] ^ 1)).astype(np.float32), BF16)


def _rope_tables(n_tokens):
    rows = n_tokens // GRID_W
    row = jnp.broadcast_to(jnp.arange(rows)[:, None], (rows, GRID_W)).reshape(-1)
    col = jnp.broadcast_to(jnp.arange(GRID_W)[None, :], (rows, GRID_W)).reshape(-1)
    n_freq = ATT_HD // 4
    inv = ROPE_BASE ** (-jnp.arange(n_freq, dtype=F32) / n_freq)
    ang = jnp.concatenate([row[:, None].astype(F32) * inv, col[:, None].astype(F32) * inv], axis=-1)
    cos = jnp.repeat(jnp.cos(ang), 2, axis=1)
    sin = jnp.repeat(jnp.sin(ang), 2, axis=1)
    sign = jnp.where(jnp.arange(ATT_HD) % 2 == 0, -1.0, 1.0).astype(F32)
    sin = sin * sign
    reps = LANE // ATT_HD
    return jnp.tile(cos, (1, reps)), jnp.tile(sin, (1, reps))


def _permute_w_in(w_in):
    o_pool = 0
    o_qkv = o_pool + POOL_WIDTH
    o_z = o_qkv + DN_CONV_CH
    o_ba = o_z + DN_VW
    o_aq = o_ba + 4 * DN_HEADS
    o_gate = o_aq + Q_W + 2 * KV_W
    pad = jnp.zeros(w_in.shape[:-1] + (BA_PAD - 4 * DN_HEADS,), w_in.dtype)
    parts = [w_in[..., o_pool:o_ba], w_in[..., o_aq:], w_in[..., o_ba:o_aq], pad]
    return jnp.concatenate(parts, axis=-1).astype(BF16)


def _head_select_const():
    sel = np.zeros((2, LANE, 2 * DN_VW), np.float32)
    for d in range(2):
        for h in range(DN_HEADS):
            sel[d, d * DN_HEADS + h, h * DN_DK:(h + 1) * DN_DK] = 1.0
            sel[d, 2 * DN_HEADS + d * DN_HEADS + h, DN_VW + h * DN_DK:DN_VW + (h + 1) * DN_DK] = 1.0
    return jnp.asarray(sel, BF16)


def _lane_vec(vals, offset):
    out = jnp.zeros((1, LANE), F32)
    return out.at[0, offset:offset + vals.size].set(vals.reshape(-1).astype(F32))


def kernel(x, c, ctx, c_ctx, w_ada, b_ada, norm_ffn1, w_ffn1_in, w_ffn1_out, norm_mix, w_in, pool_w, pool_scale, dn_conv, dn_a_log, dn_dt_bias, dn_norm, q_norm, k_norm, w_branch, w_out, norm_ffn2, w_ffn2_in, w_ffn2_out, norm_final):
    nb, seq, d = x.shape
    lc = ctx.shape[1]
    depth = w_ada.shape[0]

    n_rows = -(-(nb + 1) // SUBLANE) * SUBLANE
    cond = jnp.concatenate([c, c_ctx[None], jnp.zeros((n_rows - nb - 1, d), F32)], axis=0)
    mod_all = _ada_call(cond, w_ada, b_ada).reshape(depth, n_rows, N_MOD, 1, d)

    cos_x, sin_x = _rope_tables(seq)
    cos_c = jnp.ones((lc, LANE), F32)
    sin_c = jnp.zeros((lc, LANE), F32)
    e_q = _block_diag_const(Q_W, ATT_HD, 1.0 / ATT_HD)
    e_k = _block_diag_const(KV_W, ATT_HD, 1.0 / ATT_HD)
    p_q = _pair_swap_const(Q_W)
    p_k = _pair_swap_const(KV_W)
    e_sum = _block_diag_const(DN_QK, DN_DK, 1.0)
    e_mean = _block_diag_const(DN_VW, DN_DV, 1.0 / DN_DV)
    head_sel = _head_select_const()

    wf1_in = w_ffn1_in.astype(BF16)
    wf1_out = w_ffn1_out.astype(BF16)
    wf2_in = w_ffn2_in.astype(BF16)
    wf2_out = w_ffn2_out.astype(BF16)
    w_perm = _permute_w_in(w_in)
    w_br = w_branch.astype(BF16)
    w_o = w_out.astype(BF16)

    xc = ctx.reshape(1, nb * lc, d)
    for l in range(depth):
        ctx_out = l < depth - 1
        mod_x = mod_all[l, :nb]
        mod_c = mod_all[l, nb:nb + 1]
        qg = jnp.tile(q_norm[l], ATT_HEADS).reshape(1, Q_W)
        kg = jnp.tile(k_norm[l], ATT_KV).reshape(1, KV_W)
        pool_bd = jax.scipy.linalg.block_diag(*[pool_w[l, g] for g in range(len(POOL_WINDOWS))]).astype(BF16)
        pool_s = pool_scale[l].reshape(1, POOL_WIDTH)
        alog = _lane_vec(dn_a_log[l], 2 * DN_HEADS)
        dtb = _lane_vec(dn_dt_bias[l], 2 * DN_HEADS)
        gn = jnp.tile(dn_norm[l], DN_HEADS).reshape(1, DN_VW)

        x = _ffn_call(x, mod_x, 0, norm_ffn1[l], wf1_in, wf1_out, l, norm_final)
        xc = _ffn_call(xc, mod_c, 0, norm_ffn1[l], wf1_in, wf1_out, l, norm_final)

        proj_consts = (qg, kg, e_q, p_q, e_k, p_k, dn_conv[l], e_sum)
        xc_seq = xc.reshape(nb, lc, d)
        pool_x, qd_x, kd_x, vd_x, z_x, ba_x, aq_x, ak_x, av_x, gate_x = _proj_call(
            x, mod_x, 1, norm_mix[l], w_perm, l, cos_x, sin_x, *proj_consts)
        pool_c, qd_c, kd_c, vd_c, z_c, ba_c, aq_c, ak_c, av_c, gate_c = _proj_call(
            xc_seq, mod_c, 1, norm_mix[l], w_perm, l, cos_c, sin_c, *proj_consts)

        yd_c, yd_x = _delta_call((qd_c, kd_c, vd_c, ba_c, z_c), (qd_x, kd_x, vd_x, ba_x, z_x),
                                 alog, dtb, gn, e_mean, head_sel)

        ya_x = _attn_call(aq_x, ak_c, av_c, ak_x, av_x, seq)
        x = _merge_call(x, mod_x, 1, pool_x, yd_x, ya_x, gate_x, pool_bd, pool_s, w_br, w_o, l)

        if ctx_out:
            ya_c = _attn_call(aq_c, ak_c, av_c, ak_c, av_c, 0)
            xc = _merge_call(xc_seq, mod_c, 1, pool_c, yd_c, ya_c, gate_c, pool_bd, pool_s, w_br, w_o, l)
            xc = xc.reshape(1, nb * lc, d)

        x = _ffn_call(x, mod_x, 2, norm_ffn2[l], wf2_in, wf2_out, l, norm_final, out_norm=not ctx_out)
        if ctx_out:
            xc = _ffn_call(xc, mod_c, 2, norm_ffn2[l], wf2_in, wf2_out, l, norm_final)

    return x
```
